```python
import jax, jax.numpy as jnp
from jax import lax
import numpy as np

D_MODEL = 1024
BATCH = 1
SEQ = 16384
DEPTH = 2
DEC_BATCH = 32
DEC_SEQ = 4
PAST_LEN = 16384
PAGE_SIZE = 128

N_A_LAYERS = DEPTH // 2
N_B_LAYERS = DEPTH - N_A_LAYERS
N_DENSE_LAYERS = (DEPTH + 1) // 2
N_MOE_LAYERS = DEPTH // 2
CONV_WIDTH = 31
N_HEADS = 16
HEAD_DIM = D_MODEL // N_HEADS
GROUPS = ((128, 1), (512, 4), (2048, 16))
N_GROUPS = len(GROUPS)
MAX_WINDOW = max(w for w, _ in GROUPS)
Q_BLOCK = 128
ROPE_THETA = 10000.0
D_FF = 2816
N_EXPERTS = 8
TOP_K = 2
D_FF_EXPERT = 3584
LN_EPS = 1e-5
ALPHA = (2 * DEPTH) ** 0.25
BETA = (8 * DEPTH) ** -0.25

kernel_name = "yoco_conformer_dilated_window_decoder_step"

F32 = jnp.float32


def layer_norm(x, g, b):
    xf = x.astype(F32)
    xc = xf - jnp.mean(xf, -1, keepdims=True)
    var = jnp.mean(xc * xc, -1, keepdims=True)
    return (xc * lax.rsqrt(var + LN_EPS) * g.astype(F32) + b.astype(F32)).astype(x.dtype)


def rope(x, pos):
    half = HEAD_DIM // 2
    inv = 1.0 / (ROPE_THETA ** (jnp.arange(half, dtype=F32) * (2.0 / HEAD_DIM)))
    ang = pos.astype(F32)[:, None] * inv[None, :]
    shape = (1, pos.shape[0]) + (1,) * (x.ndim - 3) + (half,)
    cos = jnp.cos(ang).reshape(shape)
    sin = jnp.sin(ang).reshape(shape)
    xf = x.astype(F32)
    x1, x2 = xf[..., :half], xf[..., half:]
    return jnp.concatenate([x1 * cos - x2 * sin, x2 * cos + x1 * sin], -1).astype(x.dtype)


def conv_module(x, state, w_in, b_in, w_dw, b_dw, g, b, w_out, b_out):
    h = x @ w_in + b_in
    u = h[..., :D_MODEL] * jax.nn.sigmoid(h[..., D_MODEL:])
    ext = jnp.concatenate([state.astype(u.dtype), u], axis=1)
    new_state = ext[:, ext.shape[1] - (CONV_WIDTH - 1):]
    c = lax.conv_general_dilated(ext, w_dw[:, None, :].astype(ext.dtype), window_strides=(1,), padding='VALID',
                                 dimension_numbers=('NWC', 'WIO', 'NWC'), feature_group_count=D_MODEL) + b_dw
    c = jax.nn.silu(layer_norm(c, g, b))
    return c @ w_out + b_out, new_state


def swiglu(x, wg, wu, wd):
    return (jax.nn.silu(x @ wg) * (x @ wu)) @ wd


def moe_swiglu(x, wr, wg, wu, wd):
    logits = jnp.einsum('btd,de->bte', x, wr, preferred_element_type=F32)
    top_val, top_idx = lax.top_k(logits, TOP_K)
    gates = jax.nn.softmax(top_val, axis=-1)
    comb = jnp.einsum('btk,btke->bte', gates, jax.nn.one_hot(top_idx, N_EXPERTS, dtype=F32)).astype(x.dtype)
    y = jnp.zeros_like(x)
    for e in range(N_EXPERTS):
        y = y + comb[..., e:e + 1] * swiglu(x, wg[e], wu[e], wd[e])
    return y


def dilated_window_attention(q, kv, past_kv, pos0):
    bsz, t_len = q.shape[0], q.shape[1]
    past = past_kv.shape[1]
    ext = jnp.concatenate([past_kv.astype(kv.dtype), kv], axis=1)
    qb = Q_BLOCK if t_len % Q_BLOCK == 0 else t_len
    n_blocks = t_len // qb
    scale = HEAD_DIM ** -0.5
    q_local = np.arange(qb)

    def one_block(blk):
        start = blk * qb
        slab = lax.dynamic_slice_in_dim(ext, start, past + qb, axis=1)
        q_blk = lax.dynamic_slice_in_dim(q, start, qb, axis=1).astype(F32)
        q_pos = pos0 + start + jnp.arange(qb, dtype=jnp.int32)
        outs, lses = [], []
        for g, (window, dil) in enumerate(GROUPS):
            dist = np.arange(window // dil + 1) * dil
            idx = np.maximum(q_local[:, None] + past - dist[None, :], 0)
            kv_g = slab[:, idx].astype(F32)
            valid = (q_pos[:, None] - jnp.asarray(dist, jnp.int32)[None, :]) >= 0
            s = jnp.einsum('bqhd,bqjhd->bqhj', q_blk[:, :, g], kv_g[:, :, :, 0]) * scale
            s = jnp.where(valid[None, :, None, :], s, -jnp.inf)
            m = jnp.max(s, -1, keepdims=True)
            e = jnp.exp(s - m)
            den = jnp.sum(e, -1, keepdims=True)
            outs.append(jnp.einsum('bqhj,bqjhd->bqhd', e, kv_g[:, :, :, 1]) / den)
            lses.append(m + jnp.log(den))
        w = jax.nn.softmax(jnp.stack(lses, 0), axis=0)
        return jnp.sum(w * jnp.stack(outs, 0), 0)

    out = lax.map(one_block, jnp.arange(n_blocks, dtype=jnp.int32))
    out = jnp.moveaxis(out, 0, 1).reshape(bsz, t_len, N_HEADS, HEAD_DIM)
    return out.astype(q.dtype)


def trunk(x, pos0, conv_state, past_kv, p):
    bsz, t_len, _ = x.shape
    pos = pos0 + jnp.arange(t_len, dtype=jnp.int32)
    new_conv = []
    kv = None
    for layer in range(DEPTH):
        if layer < N_A_LAYERS:
            a = layer
            mix, st = conv_module(x, conv_state[a], p['conv_w_in'][a], p['conv_b_in'][a], p['conv_w_dw'][a],
                                  p['conv_b_dw'][a], p['conv_ln_g'][a], p['conv_ln_b'][a],
                                  p['conv_w_out'][a], p['conv_b_out'][a])
            new_conv.append(st)
        else:
            if layer == N_A_LAYERS:
                kv = (x @ p['w_kv']).reshape(bsz, t_len, 2, N_HEADS, HEAD_DIM)
                kv = jnp.concatenate([rope(kv[:, :, :1], pos), kv[:, :, 1:]], axis=2)
            bi = layer - N_A_LAYERS
            q = rope((x @ p['attn_w_q'][bi]).reshape(bsz, t_len, N_GROUPS, N_HEADS, HEAD_DIM), pos)
            att = dilated_window_attention(q, kv, past_kv, pos0)
            mix = att.reshape(bsz, t_len, N_HEADS * HEAD_DIM) @ p['attn_w_o'][bi]
        x = layer_norm(ALPHA * x + mix, p['ln_g'][layer, 0], p['ln_b'][layer, 0])
        if layer % 2 == 0:
            di = layer // 2
            ff = swiglu(x, p['ffn_w_gate'][di], p['ffn_w_up'][di], p['ffn_w_down'][di])
        else:
            mi = layer // 2
            ff = moe_swiglu(x, p['moe_w_router'][mi], p['moe_w_gate'][mi], p['moe_w_up'][mi], p['moe_w_down'][mi])
        x = layer_norm(ALPHA * x + ff, p['ln_g'][layer, 1], p['ln_b'][layer, 1])
    return x, jnp.stack(new_conv, 0), kv


def setup_inputs(seed: int = 0) -> dict:
    key = jax.random.key(seed)
    ks = jax.random.split(key, 24)

    def nrm(k, shape, scale):
        return jax.random.normal(k, shape, F32) * scale

    l_kv = min(MAX_WINDOW, PAST_LEN)
    d_att = N_HEADS * HEAD_DIM
    return {
        'x_prompt': nrm(ks[0], (BATCH, SEQ, D_MODEL), 1.0),
        'x_sample': nrm(ks[1], (DEC_BATCH, DEC_SEQ, D_MODEL), 1.0),
        'cache_kv': nrm(ks[2], (DEC_BATCH, l_kv, 2, N_HEADS, HEAD_DIM), 1.0),
        'state_conv': nrm(ks[3], (N_A_LAYERS, DEC_BATCH, CONV_WIDTH - 1, D_MODEL), 0.5),
        'conv_w_in': nrm(ks[4], (N_A_LAYERS, D_MODEL, 2 * D_MODEL), D_MODEL ** -0.5),
        'conv_b_in': nrm(ks[5], (N_A_LAYERS, 2 * D_MODEL), 0.02),
        'conv_w_dw': nrm(ks[6], (N_A_LAYERS, CONV_WIDTH, D_MODEL), CONV_WIDTH ** -0.5),
        'conv_b_dw': nrm(ks[7], (N_A_LAYERS, D_MODEL), 0.02),
        'conv_ln_g': 1.0 + nrm(ks[8], (N_A_LAYERS, D_MODEL), 0.02),
        'conv_ln_b': nrm(ks[9], (N_A_LAYERS, D_MODEL), 0.02),
        'conv_w_out': nrm(ks[10], (N_A_LAYERS, D_MODEL, D_MODEL), BETA * D_MODEL ** -0.5),
        'conv_b_out': nrm(ks[11], (N_A_LAYERS, D_MODEL), 0.02),
        'ffn_w_gate': nrm(ks[12], (N_DENSE_LAYERS, D_MODEL, D_FF), D_MODEL ** -0.5),
        'ffn_w_up': nrm(ks[13], (N_DENSE_LAYERS, D_MODEL, D_FF), D_MODEL ** -0.5),
        'ffn_w_down': nrm(ks[14], (N_DENSE_LAYERS, D_FF, D_MODEL), BETA * D_FF ** -0.5),
        'w_kv': nrm(ks[15], (D_MODEL, 2 * d_att), D_MODEL ** -0.5),
        'attn_w_q': nrm(ks[16], (N_B_LAYERS, D_MODEL, N_GROUPS * d_att), D_MODEL ** -0.5),
        'attn_w_o': nrm(ks[17], (N_B_LAYERS, d_att, D_MODEL), BETA * d_att ** -0.5),
        'moe_w_router': nrm(ks[18], (N_MOE_LAYERS, D_MODEL, N_EXPERTS), D_MODEL ** -0.5),
        'moe_w_gate': nrm(ks[19], (N_MOE_LAYERS, N_EXPERTS, D_MODEL, D_FF_EXPERT), D_MODEL ** -0.5),
        'moe_w_up': nrm(ks[20], (N_MOE_LAYERS, N_EXPERTS, D_MODEL, D_FF_EXPERT), D_MODEL ** -0.5),
        'moe_w_down': nrm(ks[21], (N_MOE_LAYERS, N_EXPERTS, D_FF_EXPERT, D_MODEL), BETA * D_FF_EXPERT ** -0.5),
        'ln_g': 1.0 + nrm(ks[22], (DEPTH, 2, D_MODEL), 0.02),
        'ln_b': nrm(ks[23], (DEPTH, 2, D_MODEL), 0.02),
    }


def reference(x_prompt, x_sample, cache_kv, state_conv, conv_w_in, conv_b_in, conv_w_dw, conv_b_dw, conv_ln_g,
              conv_ln_b, conv_w_out, conv_b_out, ffn_w_gate, ffn_w_up, ffn_w_down, w_kv, attn_w_q, attn_w_o,
              moe_w_router, moe_w_gate, moe_w_up, moe_w_down, ln_g, ln_b):
    p = {
        'conv_w_in': conv_w_in, 'conv_b_in': conv_b_in, 'conv_w_dw': conv_w_dw, 'conv_b_dw': conv_b_dw,
        'conv_ln_g': conv_ln_g, 'conv_ln_b': conv_ln_b, 'conv_w_out': conv_w_out, 'conv_b_out': conv_b_out,
        'ffn_w_gate': ffn_w_gate, 'ffn_w_up': ffn_w_up, 'ffn_w_down': ffn_w_down,
        'w_kv': w_kv, 'attn_w_q': attn_w_q, 'attn_w_o': attn_w_o,
        'moe_w_router': moe_w_router, 'moe_w_gate': moe_w_gate, 'moe_w_up': moe_w_up, 'moe_w_down': moe_w_down,
        'ln_g': ln_g, 'ln_b': ln_b,
    }
    bp = x_prompt.shape[0]
    conv0 = jnp.zeros((N_A_LAYERS, bp, CONV_WIDTH - 1, D_MODEL), x_prompt.dtype)
    past0 = jnp.zeros((bp, MAX_WINDOW, 2, N_HEADS, HEAD_DIM), x_prompt.dtype)
    y_prompt, conv_prompt, kv_prompt_all = trunk(x_prompt, 0, conv0, past0, p)
    t_p = kv_prompt_all.shape[1]
    kv_prompt = kv_prompt_all[:, t_p - min(MAX_WINDOW, t_p):]
    y_sample, conv_sample, kv_sample = trunk(x_sample, PAST_LEN, state_conv, cache_kv, p)
    return (y_prompt, y_sample, conv_prompt, conv_sample, kv_prompt, kv_sample)
```

```python
import functools

import numpy as np
import jax
import jax.numpy as jnp
from jax import lax
from jax.experimental import pallas as pl
from jax.experimental.pallas import tpu as pltpu

F32 = jnp.float32
BF16 = jnp.bfloat16

D_MODEL = 1024
N_HEADS = 16
HEAD_DIM = 64
GROUPS = ((128, 1), (512, 4), (2048, 16))
N_GROUPS = len(GROUPS)
KEYS_PER_GROUP = 128
CONV_WIDTH = 31
HALO = CONV_WIDTH - 1
N_EXPERTS = 8
ROPE_THETA = 10000.0
LN_EPS = 1e-5
DEPTH = 2
ALPHA = (2 * DEPTH) ** 0.25
PAST_LEN = 16384
NEG = -1e30

LANES = 128
HALO_PAD = 32
VMEM_LIMIT = 56 * 1024 * 1024


def _params(sem, vmem=VMEM_LIMIT):
    return pltpu.CompilerParams(dimension_semantics=sem, vmem_limit_bytes=vmem)


def _dot(a, b):
    return jnp.dot(a, b, preferred_element_type=F32)


def _sigmoid(x):
    return 1.0 / (1.0 + jnp.exp(-x))


def _silu(x):
    return x * _sigmoid(x)


def _ln(x, g, b):
    xc = x - jnp.mean(x, axis=-1, keepdims=True)
    var = jnp.mean(xc * xc, axis=-1, keepdims=True)
    return xc * lax.rsqrt(var + LN_EPS) * g + b


def _full(shape):
    return pl.BlockSpec(shape, lambda *_: (0,) * len(shape))


def _conv_taps(s_ref, wdw_ref, bdw_ref, c_ref, n_rows, row_chunk):
    base = HALO_PAD - HALO

    def chunk(rc, carry):
        r0 = pl.multiple_of(rc * row_chunk, row_chunk)
        win = s_ref.at[pl.ds(r0, row_chunk + HALO_PAD), :]
        for c0 in range(0, D_MODEL, LANES):
            acc = jnp.broadcast_to(bdw_ref[:, c0:c0 + LANES], (row_chunk, LANES))
            for k in range(CONV_WIDTH):
                acc = acc + wdw_ref[k:k + 1, c0:c0 + LANES] * win[base + k:base + k + row_chunk, c0:c0 + LANES]
            c_ref[pl.ds(r0, row_chunk), c0:c0 + LANES] = acc
        return carry

    lax.fori_loop(0, n_rows // row_chunk, chunk, 0)


def _conv_prompt_kernel(x_ref, st_ref, win_ref, bin_ref, wdw_ref, bdw_ref, cg_ref, cb_ref, wout_ref, bout_ref,
                        lg_ref, lb_ref, o_ref, ns_ref, s_ref, c_ref, *, tm, row_chunk):
    base = HALO_PAD - HALO

    @pl.when(pl.program_id(1) == 0)
    def _():
        s_ref[0:HALO_PAD, :] = jnp.zeros((HALO_PAD, D_MODEL), F32)
        s_ref[base:HALO_PAD, :] = st_ref[0]

    x = x_ref[0]
    h = _dot(x.astype(BF16), win_ref[...]) + bin_ref[...]
    s_ref[HALO_PAD:HALO_PAD + tm, :] = h[:, :D_MODEL] * _sigmoid(h[:, D_MODEL:])
    _conv_taps(s_ref, wdw_ref, bdw_ref, c_ref, tm, row_chunk)
    c = _silu(_ln(c_ref[...], cg_ref[...], cb_ref[...]))
    mix = _dot(c.astype(BF16), wout_ref[...]) + bout_ref[...]
    o_ref[0] = _ln(ALPHA * x + mix, lg_ref[...], lb_ref[...])
    tail = s_ref[pl.ds(tm + base, HALO), :]
    s_ref[base:HALO_PAD, :] = tail
    ns_ref[0] = tail


def _conv_layer_prompt(x, state, w_in, b_in, w_dw, b_dw, cg, cb, w_out, b_out, lg, lb, *, tm=256):
    bsz, t_len, _ = x.shape
    tm = min(tm, t_len)
    assert t_len % tm == 0 and tm >= HALO_PAD and tm % 8 == 0
    row_chunk = 64 if tm % 64 == 0 else 8
    row = lambda n: _full((1, n))
    return pl.pallas_call(
        functools.partial(_conv_prompt_kernel, tm=tm, row_chunk=row_chunk),
        grid=(bsz, t_len // tm),
        in_specs=[
            pl.BlockSpec((1, tm, D_MODEL), lambda b, i: (b, i, 0)),
            pl.BlockSpec((1, HALO, D_MODEL), lambda b, i: (b, 0, 0)),
            _full((D_MODEL, 2 * D_MODEL)), row(2 * D_MODEL),
            _full((CONV_WIDTH, D_MODEL)), row(D_MODEL), row(D_MODEL), row(D_MODEL),
            _full((D_MODEL, D_MODEL)), row(D_MODEL), row(D_MODEL), row(D_MODEL),
        ],
        out_specs=[
            pl.BlockSpec((1, tm, D_MODEL), lambda b, i: (b, i, 0)),
            pl.BlockSpec((1, HALO, D_MODEL), lambda b, i: (b, 0, 0)),
        ],
        out_shape=[jax.ShapeDtypeStruct(x.shape, F32), jax.ShapeDtypeStruct((bsz, HALO, D_MODEL), F32)],
        scratch_shapes=[pltpu.VMEM((HALO_PAD + tm, D_MODEL), F32), pltpu.VMEM((tm, D_MODEL), F32)],
        compiler_params=_params(("arbitrary", "arbitrary")),
        name="conv_layer_prompt",
    )(x, state, w_in, b_in, w_dw, b_dw, cg, cb, w_out, b_out, lg, lb)


def _conv_sample_kernel(x_ref, st_ref, win_ref, bin_ref, wdw_ref, bdw_ref, cg_ref, cb_ref, wout_ref, bout_ref,
                        lg_ref, lb_ref, o_ref, ns_ref, s_ref, u_ref, c_ref, *, bsz, t_len):
    base = HALO_PAD - HALO
    x = x_ref[...]
    h = _dot(x.astype(BF16), win_ref[...]) + bin_ref[...]
    u_ref[...] = h[:, :D_MODEL] * _sigmoid(h[:, D_MODEL:])
    s_ref[:, 0:HALO_PAD, :] = jnp.zeros((bsz, HALO_PAD, D_MODEL), F32)
    s_ref[:, base:HALO_PAD, :] = st_ref[...]
    for b in range(bsz):
        s_ref[b, HALO_PAD:HALO_PAD + t_len, :] = u_ref[b * t_len:(b + 1) * t_len, :]
    acc = jnp.broadcast_to(bdw_ref[...][None], (bsz, t_len, D_MODEL))
    for k in range(CONV_WIDTH):
        acc = acc + wdw_ref[k:k + 1, :][None] * s_ref[:, base + k:base + k + t_len, :]
    for b in range(bsz):
        c_ref[b * t_len:(b + 1) * t_len, :] = acc[b]
    c = _silu(_ln(c_ref[...], cg_ref[...], cb_ref[...]))
    mix = _dot(c.astype(BF16), wout_ref[...]) + bout_ref[...]
    o_ref[...] = _ln(ALPHA * x + mix, lg_ref[...], lb_ref[...])
    ns_ref[...] = s_ref[:, base + t_len:HALO_PAD + t_len, :]


def _conv_layer_sample(x, state, w_in, b_in, w_dw, b_dw, cg, cb, w_out, b_out, lg, lb):
    bsz, t_len, _ = x.shape
    n = bsz * t_len
    assert t_len <= 8
    out, ns = pl.pallas_call(
        functools.partial(_conv_sample_kernel, bsz=bsz, t_len=t_len),
        out_shape=[jax.ShapeDtypeStruct((n, D_MODEL), F32), jax.ShapeDtypeStruct((bsz, HALO, D_MODEL), F32)],
        scratch_shapes=[pltpu.VMEM((bsz, HALO_PAD + 8, D_MODEL), F32), pltpu.VMEM((n, D_MODEL), F32),
                        pltpu.VMEM((n, D_MODEL), F32)],
        compiler_params=pltpu.CompilerParams(vmem_limit_bytes=VMEM_LIMIT),
        name="conv_layer_sample",
    )(x.reshape(n, D_MODEL), state, w_in, b_in, w_dw, b_dw, cg, cb, w_out, b_out, lg, lb)
    return out.reshape(x.shape), ns


def _ffn_kernel(x_ref, wg_ref, wu_ref, wd_ref, g_ref, b_ref, o_ref, acc_ref, xb_ref):
    f = pl.program_id(1)

    @pl.when(f == 0)
    def _():
        acc_ref[...] = jnp.zeros_like(acc_ref)
        xb_ref[...] = x_ref[...].astype(BF16)

    xb = xb_ref[...]
    h = _silu(_dot(xb, wg_ref[...])) * _dot(xb, wu_ref[...])
    acc_ref[...] += _dot(h.astype(BF16), wd_ref[...])

    @pl.when(f == pl.num_programs(1) - 1)
    def _():
        o_ref[...] = _ln(ALPHA * x_ref[...] + acc_ref[...], g_ref[...], b_ref[...])


def _ffn_layer(x, wg, wu, wd, g, b, *, tm=512, n_f=2):
    n = x.shape[0]
    d_ff = wg.shape[1]
    tm = min(tm, n)
    tf = d_ff // n_f
    assert n % tm == 0 and d_ff % n_f == 0 and tf % LANES == 0
    return pl.pallas_call(
        _ffn_kernel,
        grid=(n // tm, n_f),
        in_specs=[
            pl.BlockSpec((tm, D_MODEL), lambda i, f: (i, 0)),
            pl.BlockSpec((D_MODEL, tf), lambda i, f: (0, f)),
            pl.BlockSpec((D_MODEL, tf), lambda i, f: (0, f)),
            pl.BlockSpec((tf, D_MODEL), lambda i, f: (f, 0)),
            _full((1, D_MODEL)), _full((1, D_MODEL)),
        ],
        out_specs=pl.BlockSpec((tm, D_MODEL), lambda i, f: (i, 0)),
        out_shape=jax.ShapeDtypeStruct((n, D_MODEL), F32),
        scratch_shapes=[pltpu.VMEM((tm, D_MODEL), F32), pltpu.VMEM((tm, D_MODEL), BF16)],
        compiler_params=_params(("arbitrary", "arbitrary")),
        name="ffn_dense",
    )(x, wg, wu, wd, g, b)


def _rope_tables(pos):
    half = HEAD_DIM // 2
    inv = 1.0 / (ROPE_THETA ** (jnp.arange(half, dtype=F32) * (2.0 / HEAD_DIM)))
    ang = pos.astype(F32)[:, None] * inv[None, :]
    cos, sin = jnp.cos(ang), jnp.sin(ang)
    return jnp.tile(cos, (1, 4)), jnp.tile(jnp.concatenate([-sin, sin], axis=1), (1, 2))


def _rope(x, cos, sin, first_half):
    half = HEAD_DIM // 2
    out = []
    for c0 in range(0, x.shape[1], LANES):
        slab = x[:, c0:c0 + LANES]
        partner = jnp.where(first_half, pltpu.roll(slab, LANES - half, 1), pltpu.roll(slab, half, 1))
        out.append(slab * cos + partner * sin)
    return jnp.concatenate(out, axis=1)


def _qkv_kernel(x_ref, cos_ref, sin_ref, wkv_ref, wq_ref, kv_ref, k_ref, v_ref, q_ref):
    xb = x_ref[...].astype(BF16)
    cos, sin = cos_ref[...], sin_ref[...]
    first_half = (lax.broadcasted_iota(jnp.int32, cos.shape, 1) % HEAD_DIM) < HEAD_DIM // 2
    d_att = N_HEADS * HEAD_DIM
    kv = _dot(xb, wkv_ref[...])
    k = _rope(kv[:, :d_att], cos, sin, first_half)
    v = kv[:, d_att:]
    kv_ref[:, :d_att] = k
    kv_ref[:, d_att:] = v
    k_ref[...] = k.astype(BF16)
    v_ref[...] = v.astype(BF16)
    q = _rope(_dot(xb, wq_ref[...]), cos, sin, first_half) * (HEAD_DIM ** -0.5)
    q_ref[...] = q.astype(BF16)


def _qkv_layer(x, pos, w_kv, w_q, *, tm=256):
    n = x.shape[0]
    tm = min(tm, n)
    assert n % tm == 0
    d_att = N_HEADS * HEAD_DIM
    cos, sin = _rope_tables(pos)
    return pl.pallas_call(
        _qkv_kernel,
        grid=(n // tm,),
        in_specs=[
            pl.BlockSpec((tm, D_MODEL), lambda i: (i, 0)),
            pl.BlockSpec((tm, LANES), lambda i: (i, 0)),
            pl.BlockSpec((tm, LANES), lambda i: (i, 0)),
            _full((D_MODEL, 2 * d_att)), _full((D_MODEL, N_GROUPS * d_att)),
        ],
        out_specs=[
            pl.BlockSpec((tm, 2 * d_att), lambda i: (i, 0)),
            pl.BlockSpec((tm, d_att), lambda i: (i, 0)),
            pl.BlockSpec((tm, d_att), lambda i: (i, 0)),
            pl.BlockSpec((tm, N_GROUPS * d_att), lambda i: (i, 0)),
        ],
        out_shape=[jax.ShapeDtypeStruct((n, 2 * d_att), F32), jax.ShapeDtypeStruct((n, d_att), BF16),
                   jax.ShapeDtypeStruct((n, d_att), BF16), jax.ShapeDtypeStruct((n, N_GROUPS * d_att), BF16)],
        compiler_params=_params(("arbitrary",)),
        name="qkv_rope",
    )(x, cos, sin, w_kv, w_q)


def _attn_prompt_kernel(q_ref, kp_ref, kc_ref, vp_ref, vc_ref, o_ref, lse_ref, *, qb):
    blk = pl.program_id(1)
    a = lax.broadcasted_iota(jnp.int32, (qb, 2 * qb), 0)
    c = lax.broadcasted_iota(jnp.int32, (qb, 2 * qb), 1)
    valid = (c >= a) & (c <= a + KEYS_PER_GROUP) & ((blk > 0) | (c >= qb))
    lane = lax.broadcasted_iota(jnp.int32, (qb, LANES), 1)
    low = lane < HEAD_DIM
    lse_all = jnp.zeros((qb, LANES), F32)
    for p in range(N_HEADS // 2):
        sl = slice(p * LANES, (p + 1) * LANES)
        qp = q_ref[:, sl]
        kk = jnp.concatenate([kp_ref[:, sl], kc_ref[:, sl]], axis=0)
        vv = jnp.concatenate([vp_ref[:, sl], vc_ref[:, sl]], axis=0)
        outs = []
        for j in range(2):
            qm = jnp.where(low if j == 0 else jnp.logical_not(low), qp, jnp.zeros_like(qp))
            s = lax.dot_general(qm, kk, (((1,), (1,)), ((), ())), preferred_element_type=F32)
            s = jnp.where(valid, s, NEG)
            m = jnp.max(s, axis=1, keepdims=True)
            e = jnp.exp(s - m)
            den = jnp.sum(e, axis=1, keepdims=True)
            outs.append(_dot(e.astype(BF16), vv) / den)
            lse_all = jnp.where(lane == 2 * p + j, m + jnp.log(den), lse_all)
        o_ref[:, sl] = jnp.where(low, outs[0], outs[1]).astype(BF16)
    lse_ref[0] = lse_all


def _attn_prompt_group(q, k, v, g, *, qb=128):
    t_len = k.shape[0]
    dil = GROUPS[g][1]
    assert GROUPS[g][0] // dil == KEYS_PER_GROUP and qb == KEYS_PER_GROUP
    rows = t_len // dil
    assert t_len % dil == 0 and rows % qb == 0
    d_att = N_HEADS * HEAD_DIM
    qg = q[:, g * d_att:(g + 1) * d_att].reshape(rows, dil * d_att)
    kg = k.reshape(rows, dil * d_att)
    vg = v.reshape(rows, dil * d_att)
    cur = pl.BlockSpec((qb, d_att), lambda r, b: (b, r))
    prev = pl.BlockSpec((qb, d_att), lambda r, b: (jnp.maximum(b - 1, 0), r))
    out, lse = pl.pallas_call(
        functools.partial(_attn_prompt_kernel, qb=qb),
        grid=(dil, rows // qb),
        in_specs=[cur, prev, cur, prev, cur],
        out_specs=[cur, pl.BlockSpec((1, qb, LANES), lambda r, b: (r, b, 0))],
        out_shape=[jax.ShapeDtypeStruct((rows, dil * d_att), BF16), jax.ShapeDtypeStruct((dil, rows, LANES), F32)],
        compiler_params=_params(("arbitrary", "arbitrary")),
        name=f"attn_prompt_g{g}",
    )(qg, kg, kg, vg, vg)
    return out.reshape(t_len, d_att), jnp.transpose(lse, (1, 0, 2)).reshape(t_len, LANES)


def _attn_sample_kernel(q_ref, kvn_ref, cache_ref, o_ref, kb_ref, vb_ref, *, t_len, past, n_keys):
    d_att = N_HEADS * HEAD_DIM
    n_rows = 16
    n_cols = N_HEADS * n_rows
    kb_ref[0:past, :] = cache_ref[0, :, :d_att].astype(BF16)
    vb_ref[0:past, :] = cache_ref[0, :, d_att:].astype(BF16)
    pad = n_keys - past
    kn = jnp.concatenate([kvn_ref[0], jnp.zeros((pad - t_len, 2 * d_att), F32)], axis=0)
    kb_ref[past:n_keys, :] = kn[:, :d_att].astype(BF16)
    vb_ref[past:n_keys, :] = kn[:, d_att:].astype(BF16)
    q = q_ref[0].astype(F32)
    row = lax.broadcasted_iota(jnp.int32, (N_HEADS, d_att), 0)
    lane = lax.broadcasted_iota(jnp.int32, (N_HEADS, d_att), 1)
    own = row == lane // HEAD_DIM
    blocks = []
    for r in range(n_rows):
        g, i = divmod(r, t_len)
        if g < N_GROUPS:
            qr = q[i:i + 1, g * d_att:(g + 1) * d_att]
            blocks.append(jnp.where(own, jnp.broadcast_to(qr, (N_HEADS, d_att)), 0.0))
        else:
            blocks.append(jnp.zeros((N_HEADS, d_att), F32))
    qbd = jnp.concatenate(blocks, axis=0).astype(BF16)
    s = lax.dot_general(kb_ref[...], qbd, (((1,), (1,)), ((), ())), preferred_element_type=F32)
    n_idx = lax.broadcasted_iota(jnp.int32, (n_keys, n_cols), 0)
    r_idx = lax.broadcasted_iota(jnp.int32, (n_keys, n_cols), 1) // N_HEADS
    g_idx = r_idx // t_len
    dist = past + (r_idx % t_len) - n_idx
    dil = jnp.where(g_idx == 0, GROUPS[0][1], jnp.where(g_idx == 1, GROUPS[1][1], GROUPS[2][1]))
    valid = (g_idx < N_GROUPS) & (dist >= 0) & (dist <= KEYS_PER_GROUP * dil) & ((dist & (dil - 1)) == 0)
    s = jnp.where(valid, s, NEG)
    m = jnp.max(s, axis=0, keepdims=True)
    quarter = n_cols // 4
    m = jnp.maximum(jnp.maximum(m, pltpu.roll(m, quarter, 1)),
                    jnp.maximum(pltpu.roll(m, 2 * quarter, 1), pltpu.roll(m, 3 * quarter, 1)))
    e = jnp.exp(s - m)
    den = jnp.sum(e, axis=0, keepdims=True)
    den = den + pltpu.roll(den, quarter, 1) + pltpu.roll(den, 2 * quarter, 1) + pltpu.roll(den, 3 * quarter, 1)
    p = (e / den).T.astype(BF16)
    o = _dot(p, vb_ref[...]).reshape(n_rows, N_HEADS, d_att)
    o = jnp.sum(jnp.where(own[None], o, 0.0), axis=1)
    out = o[0:t_len]
    for g in range(1, N_GROUPS):
        out = out + o[g * t_len:(g + 1) * t_len]
    o_ref[0] = out


def _attn_sample(q, kv_new, cache):
    bsz, t_len, _ = q.shape
    past = cache.shape[1]
    d_att = N_HEADS * HEAD_DIM
    assert N_GROUPS * t_len <= 16 and t_len == 4 and past % LANES == 0
    assert PAST_LEN >= past and past >= max(w for w, _ in GROUPS)
    n_keys = past + LANES
    return pl.pallas_call(
        functools.partial(_attn_sample_kernel, t_len=t_len, past=past, n_keys=n_keys),
        grid=(bsz,),
        in_specs=[
            pl.BlockSpec((1, t_len, N_GROUPS * d_att), lambda b: (b, 0, 0)),
            pl.BlockSpec((1, t_len, 2 * d_att), lambda b: (b, 0, 0)),
            pl.BlockSpec((1, past, 2 * d_att), lambda b: (b, 0, 0)),
        ],
        out_specs=pl.BlockSpec((1, t_len, d_att), lambda b: (b, 0, 0)),
        out_shape=jax.ShapeDtypeStruct((bsz, t_len, d_att), F32),
        scratch_shapes=[pltpu.VMEM((n_keys, d_att), BF16), pltpu.VMEM((n_keys, d_att), BF16)],
        compiler_params=_params(("arbitrary",), 60 * 1024 * 1024),
        name="attn_sample",
    )(q, kv_new, cache)


def _split3(w):
    hi = w.astype(BF16)
    r1 = w - hi.astype(F32)
    mid = r1.astype(BF16)
    lo = (r1 - mid.astype(F32)).astype(BF16)
    return hi, mid, lo


def _oproj_router_tail(att, x_ref, wo_ref, g_ref, b_ref, wrh_ref, wrl_ref, x3_ref, comb_ref):
    x3 = _ln(ALPHA * x_ref[...] + _dot(att.astype(BF16), wo_ref[...]), g_ref[...], b_ref[...])
    x3_ref[...] = x3
    xh = x3.astype(BF16)
    xl = (x3 - xh.astype(F32)).astype(BF16)
    logits = _dot(xh, wrh_ref[...]) + (_dot(xh, wrl_ref[...]) + _dot(xl, wrh_ref[...]))
    lane = lax.broadcasted_iota(jnp.int32, logits.shape, 1)
    logits = jnp.where(lane < N_EXPERTS, logits, NEG)
    v1 = jnp.max(logits, axis=1, keepdims=True)
    i1 = jnp.min(jnp.where(logits == v1, lane, LANES), axis=1, keepdims=True)
    rest = jnp.where(lane == i1, NEG, logits)
    v2 = jnp.max(rest, axis=1, keepdims=True)
    i2 = jnp.min(jnp.where(rest == v2, lane, LANES), axis=1, keepdims=True)
    e2 = jnp.exp(v2 - v1)
    den = 1.0 + e2
    comb_ref[...] = jnp.where(lane == i1, 1.0 / den, 0.0) + jnp.where(lane == i2, e2 / den, 0.0)


def _oproj_prompt_kernel(o0_ref, o1_ref, o2_ref, lse_ref, ex_ref, x_ref, wo_ref, g_ref, b_ref, wrh_ref, wrl_ref,
                         x3_ref, comb_ref):
    lse = [lse_ref[g] for g in range(N_GROUPS)]
    m = jnp.maximum(jnp.maximum(lse[0], lse[1]), lse[2])
    w = [jnp.exp(l - m) for l in lse]
    den = w[0] + w[1] + w[2]
    att = None
    for g, o_ref in enumerate((o0_ref, o1_ref, o2_ref)):
        hi, mid, lo = _split3(w[g] / den)
        wexp = _dot(hi, ex_ref[...]) + (_dot(mid, ex_ref[...]) + _dot(lo, ex_ref[...]))
        term = wexp * o_ref[...].astype(F32)
        att = term if att is None else att + term
    _oproj_router_tail(att, x_ref, wo_ref, g_ref, b_ref, wrh_ref, wrl_ref, x3_ref, comb_ref)


def _oproj_sample_kernel(att_ref, x_ref, wo_ref, g_ref, b_ref, wrh_ref, wrl_ref, x3_ref, comb_ref):
    _oproj_router_tail(att_ref[...], x_ref, wo_ref, g_ref, b_ref, wrh_ref, wrl_ref, x3_ref, comb_ref)


def _router_split(w_router):
    wr = jnp.zeros((D_MODEL, LANES), F32).at[:, :N_EXPERTS].set(w_router)
    hi = wr.astype(BF16)
    return hi, (wr - hi.astype(F32)).astype(BF16)


def _oproj_router(att_parts, x, w_o, g, b, w_router, *, tm=256):
    n = x.shape[0]
    tm = min(tm, n)
    assert n % tm == 0
    d_att = N_HEADS * HEAD_DIM
    wrh, wrl = _router_split(w_router)
    row_blk = lambda w: pl.BlockSpec((tm, w), lambda i: (i, 0))
    tail_specs = [row_blk(D_MODEL), _full((d_att, D_MODEL)), _full((1, D_MODEL)), _full((1, D_MODEL)),
                  _full((D_MODEL, LANES)), _full((D_MODEL, LANES))]
    tail_args = (x, w_o, g, b, wrh, wrl)
    if len(att_parts) == 1:
        kern, specs, args = _oproj_sample_kernel, [row_blk(d_att)], tuple(att_parts)
    else:
        outs, lse = att_parts
        expand = (np.arange(LANES)[:, None] == np.arange(d_att)[None, :] // HEAD_DIM).astype(np.float32)
        kern = _oproj_prompt_kernel
        specs = [row_blk(d_att)] * N_GROUPS + [pl.BlockSpec((N_GROUPS, tm, LANES), lambda i: (0, i, 0)),
                                               _full((LANES, d_att))]
        args = tuple(outs) + (lse, jnp.asarray(expand, BF16))
    return pl.pallas_call(
        kern,
        grid=(n // tm,),
        in_specs=specs + tail_specs,
        out_specs=[row_blk(D_MODEL), row_blk(LANES)],
        out_shape=[jax.ShapeDtypeStruct((n, D_MODEL), F32), jax.ShapeDtypeStruct((n, LANES), F32)],
        compiler_params=_params(("arbitrary",)),
        name="oproj_router",
    )(*args, *tail_args)


def _moe_kernel(x_ref, comb_ref, wg_ref, wu_ref, wd_ref, g_ref, b_ref, o_ref, acc_ref, xb_ref):
    e, f = pl.program_id(1), pl.program_id(2)

    @pl.when((e == 0) & (f == 0))
    def _():
        acc_ref[...] = jnp.zeros_like(acc_ref)
        xb_ref[...] = x_ref[...].astype(BF16)

    xb = xb_ref[...]
    h = _silu(_dot(xb, wg_ref[...])) * _dot(xb, wu_ref[...])
    comb = comb_ref[...]
    lane = lax.broadcasted_iota(jnp.int32, comb.shape, 1)
    ce = jnp.sum(jnp.where(lane == e, comb, 0.0), axis=1, keepdims=True)
    acc_ref[...] += ce * _dot(h.astype(BF16), wd_ref[...])

    @pl.when((e == pl.num_programs(1) - 1) & (f == pl.num_programs(2) - 1))
    def _():
        o_ref[...] = _ln(ALPHA * x_ref[...] + acc_ref[...], g_ref[...], b_ref[...])


def _moe_layer(x, comb, wg, wu, wd, g, b, *, tm=512, n_f=4):
    n = x.shape[0]
    n_exp, _, d_ff = wg.shape
    tm = min(tm, n)
    tf = d_ff // n_f
    assert n % tm == 0 and d_ff % n_f == 0 and tf % LANES == 0
    return pl.pallas_call(
        _moe_kernel,
        grid=(n // tm, n_exp, n_f),
        in_specs=[
            pl.BlockSpec((tm, D_MODEL), lambda i, e, f: (i, 0)),
            pl.BlockSpec((tm, LANES), lambda i, e, f: (i, 0)),
            pl.BlockSpec((None, D_MODEL, tf), lambda i, e, f: (e, 0, f)),
            pl.BlockSpec((None, D_MODEL, tf), lambda i, e, f: (e, 0, f)),
            pl.BlockSpec((None, tf, D_MODEL), lambda i, e, f: (e, f, 0)),
            _full((1, D_MODEL)), _full((1, D_MODEL)),
        ],
        out_specs=pl.BlockSpec((tm, D_MODEL), lambda i, e, f: (i, 0)),
        out_shape=jax.ShapeDtypeStruct((n, D_MODEL), F32),
        scratch_shapes=[pltpu.VMEM((tm, D_MODEL), F32), pltpu.VMEM((tm, D_MODEL), BF16)],
        compiler_params=_params(("arbitrary", "arbitrary", "arbitrary")),
        name="moe_dense",
    )(x, comb, wg, wu, wd, g, b)


def kernel(x_prompt, x_sample, cache_kv, state_conv, conv_w_in, conv_b_in, conv_w_dw, conv_b_dw, conv_ln_g,
           conv_ln_b, conv_w_out, conv_b_out, ffn_w_gate, ffn_w_up, ffn_w_down, w_kv, attn_w_q, attn_w_o,
           moe_w_router, moe_w_gate, moe_w_up, moe_w_down, ln_g, ln_b):
    bp, t_p, _ = x_prompt.shape
    bs, t_s, _ = x_sample.shape
    assert bp == 1 and conv_w_in.shape[0] == 1 and attn_w_q.shape[0] == 1
    d_att = N_HEADS * HEAD_DIM
    row = lambda a: a.reshape(1, -1)
    bf = lambda a: a.astype(BF16)

    conv_w = (bf(conv_w_in[0]), row(conv_b_in[0]), conv_w_dw[0], row(conv_b_dw[0]), row(conv_ln_g[0]),
              row(conv_ln_b[0]), bf(conv_w_out[0]), row(conv_b_out[0]), row(ln_g[0, 0]), row(ln_b[0, 0]))
    ffn_w = (bf(ffn_w_gate[0]), bf(ffn_w_up[0]), bf(ffn_w_down[0]), row(ln_g[0, 1]), row(ln_b[0, 1]))
    wkv_b, wq_b, wo_b = bf(w_kv), bf(attn_w_q[0]), bf(attn_w_o[0])
    moe_w = (bf(moe_w_gate[0]), bf(moe_w_up[0]), bf(moe_w_down[0]), row(ln_g[1, 1]), row(ln_b[1, 1]))
    ln10 = (row(ln_g[1, 0]), row(ln_b[1, 0]))

    x1, conv_p = _conv_layer_prompt(x_prompt, jnp.zeros((bp, HALO, D_MODEL), F32), *conv_w)
    x2 = _ffn_layer(x1.reshape(t_p, D_MODEL), *ffn_w)
    kv_p, k_p, v_p, q_p = _qkv_layer(x2, jnp.arange(t_p, dtype=jnp.int32), wkv_b, wq_b)
    parts = [_attn_prompt_group(q_p, k_p, v_p, g) for g in range(N_GROUPS)]
    x3, comb = _oproj_router(([o for o, _ in parts], jnp.stack([l for _, l in parts], 0)), x2, wo_b, *ln10,
                             moe_w_router[0])
    y_prompt = _moe_layer(x3, comb, *moe_w).reshape(bp, t_p, D_MODEL)
    n_keep = min(max(w for w, _ in GROUPS), t_p)
    kv_prompt = kv_p[t_p - n_keep:].reshape(bp, n_keep, 2, N_HEADS, HEAD_DIM)

    n_s = bs * t_s
    x1s, conv_s = _conv_layer_sample(x_sample, state_conv[0], *conv_w)
    x2s = _ffn_layer(x1s.reshape(n_s, D_MODEL), *ffn_w)
    pos_s = PAST_LEN + jnp.tile(jnp.arange(t_s, dtype=jnp.int32), bs)
    kv_s, _, _, q_s = _qkv_layer(x2s, pos_s, wkv_b, wq_b)
    att_s = _attn_sample(q_s.reshape(bs, t_s, N_GROUPS * d_att), kv_s.reshape(bs, t_s, 2 * d_att),
                         cache_kv.reshape(bs, cache_kv.shape[1], 2 * d_att))
    x3s, comb_s = _oproj_router((att_s.reshape(n_s, d_att),), x2s, wo_b, *ln10, moe_w_router[0])
    y_sample = _moe_layer(x3s, comb_s, *moe_w).reshape(bs, t_s, D_MODEL)
    kv_sample = kv_s.reshape(bs, t_s, 2, N_HEADS, HEAD_DIM)

    return (y_prompt, y_sample, conv_p[None], conv_s[None], kv_prompt, kv_sample)
```

```python
import functools

import numpy as np
import jax
import jax.numpy as jnp
from jax import lax
from jax.experimental import pallas as pl
from jax.experimental.pallas import tpu as pltpu

F32 = jnp.float32
BF16 = jnp.bfloat16

D_MODEL = 1024
N_HEADS = 16
HEAD_DIM = 64
GROUPS = ((128, 1), (512, 4), (2048, 16))
N_GROUPS = len(GROUPS)
KEYS_PER_GROUP = 128
CONV_WIDTH = 31
HALO = CONV_WIDTH - 1
N_EXPERTS = 8
ROPE_THETA = 10000.0
LN_EPS = 1e-5
DEPTH = 2
ALPHA = (2 * DEPTH) ** 0.25
PAST_LEN = 16384
NEG = -1e30

LANES = 128
HALO_PAD = 32
TILE_ROWS = 8
MOE_TILE = 512
VMEM_LIMIT = 56 * 1024 * 1024


def _params(sem, vmem=VMEM_LIMIT):
    return pltpu.CompilerParams(dimension_semantics=sem, vmem_limit_bytes=vmem)


def _dot(a, b):
    return jnp.dot(a, b, preferred_element_type=F32)


def _sigmoid(x):
    return 1.0 / (1.0 + jnp.exp(-x))


def _silu(x):
    return x * _sigmoid(x)


def _ln(x, g, b):
    xc = x - jnp.mean(x, axis=-1, keepdims=True)
    var = jnp.mean(xc * xc, axis=-1, keepdims=True)
    return xc * lax.rsqrt(var + LN_EPS) * g + b


def _full(shape):
    return pl.BlockSpec(shape, lambda *_: (0,) * len(shape))


def _conv_taps(s_ref, wdw_ref, bdw_ref, c_ref, n_rows, row_chunk):
    base = HALO_PAD - HALO

    def chunk(rc, carry):
        r0 = pl.multiple_of(rc * row_chunk, row_chunk)
        win = s_ref.at[pl.ds(r0, row_chunk + HALO_PAD), :]
        for c0 in range(0, D_MODEL, LANES):
            acc = jnp.broadcast_to(bdw_ref[:, c0:c0 + LANES], (row_chunk, LANES))
            for k in range(CONV_WIDTH):
                acc = acc + wdw_ref[k:k + 1, c0:c0 + LANES] * win[base + k:base + k + row_chunk, c0:c0 + LANES]
            c_ref[pl.ds(r0, row_chunk), c0:c0 + LANES] = acc
        return carry

    lax.fori_loop(0, n_rows // row_chunk, chunk, 0)


def _conv_prompt_kernel(x_ref, st_ref, win_ref, bin_ref, wdw_ref, bdw_ref, cg_ref, cb_ref, wout_ref, bout_ref,
                        lg_ref, lb_ref, o_ref, ns_ref, s_ref, c_ref, *, tm, row_chunk):
    base = HALO_PAD - HALO

    @pl.when(pl.program_id(1) == 0)
    def _():
        s_ref[0:HALO_PAD, :] = jnp.zeros((HALO_PAD, D_MODEL), F32)
        s_ref[base:HALO_PAD, :] = st_ref[0]

    x = x_ref[0]
    h = _dot(x.astype(BF16), win_ref[...]) + bin_ref[...]
    s_ref[HALO_PAD:HALO_PAD + tm, :] = h[:, :D_MODEL] * _sigmoid(h[:, D_MODEL:])
    _conv_taps(s_ref, wdw_ref, bdw_ref, c_ref, tm, row_chunk)
    c = _silu(_ln(c_ref[...], cg_ref[...], cb_ref[...]))
    mix = _dot(c.astype(BF16), wout_ref[...]) + bout_ref[...]
    o_ref[0] = _ln(ALPHA * x + mix, lg_ref[...], lb_ref[...])
    tail = s_ref[pl.ds(tm + base, HALO), :]
    s_ref[base:HALO_PAD, :] = tail
    ns_ref[0] = tail


def _conv_layer_prompt(x, state, w_in, b_in, w_dw, b_dw, cg, cb, w_out, b_out, lg, lb, *, tm=256):
    bsz, t_len, _ = x.shape
    tm = min(tm, t_len)
    assert t_len % tm == 0 and tm >= HALO_PAD and tm % 8 == 0
    row_chunk = 64 if tm % 64 == 0 else 8
    row = lambda n: _full((1, n))
    return pl.pallas_call(
        functools.partial(_conv_prompt_kernel, tm=tm, row_chunk=row_chunk),
        grid=(bsz, t_len // tm),
        in_specs=[
            pl.BlockSpec((1, tm, D_MODEL), lambda b, i: (b, i, 0)),
            pl.BlockSpec((1, HALO, D_MODEL), lambda b, i: (b, 0, 0)),
            _full((D_MODEL, 2 * D_MODEL)), row(2 * D_MODEL),
            _full((CONV_WIDTH, D_MODEL)), row(D_MODEL), row(D_MODEL), row(D_MODEL),
            _full((D_MODEL, D_MODEL)), row(D_MODEL), row(D_MODEL), row(D_MODEL),
        ],
        out_specs=[
            pl.BlockSpec((1, tm, D_MODEL), lambda b, i: (b, i, 0)),
            pl.BlockSpec((1, HALO, D_MODEL), lambda b, i: (b, 0, 0)),
        ],
        out_shape=[jax.ShapeDtypeStruct(x.shape, F32), jax.ShapeDtypeStruct((bsz, HALO, D_MODEL), F32)],
        scratch_shapes=[pltpu.VMEM((HALO_PAD + tm, D_MODEL), F32), pltpu.VMEM((tm, D_MODEL), F32)],
        compiler_params=_params(("arbitrary", "arbitrary")),
        name="conv_layer_prompt",
    )(x, state, w_in, b_in, w_dw, b_dw, cg, cb, w_out, b_out, lg, lb)


def _conv_sample_kernel(x_ref, st_ref, win_ref, bin_ref, wdw_ref, bdw_ref, cg_ref, cb_ref, wout_ref, bout_ref,
                        lg_ref, lb_ref, o_ref, ns_ref, s_ref, u_ref, c_ref, *, bsz, t_len):
    base = HALO_PAD - HALO
    x = x_ref[...]
    h = _dot(x.astype(BF16), win_ref[...]) + bin_ref[...]
    u_ref[...] = h[:, :D_MODEL] * _sigmoid(h[:, D_MODEL:])
    s_ref[:, 0:HALO_PAD, :] = jnp.zeros((bsz, HALO_PAD, D_MODEL), F32)
    s_ref[:, base:HALO_PAD, :] = st_ref[...]
    for b in range(bsz):
        s_ref[b, HALO_PAD:HALO_PAD + t_len, :] = u_ref[b * t_len:(b + 1) * t_len, :]
    acc = jnp.broadcast_to(bdw_ref[...][None], (bsz, t_len, D_MODEL))
    for k in range(CONV_WIDTH):
        acc = acc + wdw_ref[k:k + 1, :][None] * s_ref[:, base + k:base + k + t_len, :]
    for b in range(bsz):
        c_ref[b * t_len:(b + 1) * t_len, :] = acc[b]
    c = _silu(_ln(c_ref[...], cg_ref[...], cb_ref[...]))
    mix = _dot(c.astype(BF16), wout_ref[...]) + bout_ref[...]
    o_ref[...] = _ln(ALPHA * x + mix, lg_ref[...], lb_ref[...])
    ns_ref[...] = s_ref[:, base + t_len:HALO_PAD + t_len, :]


def _conv_layer_sample(x, state, w_in, b_in, w_dw, b_dw, cg, cb, w_out, b_out, lg, lb):
    bsz, t_len, _ = x.shape
    n = bsz * t_len
    assert t_len <= 8
    out, ns = pl.pallas_call(
        functools.partial(_conv_sample_kernel, bsz=bsz, t_len=t_len),
        out_shape=[jax.ShapeDtypeStruct((n, D_MODEL), F32), jax.ShapeDtypeStruct((bsz, HALO, D_MODEL), F32)],
        scratch_shapes=[pltpu.VMEM((bsz, HALO_PAD + 8, D_MODEL), F32), pltpu.VMEM((n, D_MODEL), F32),
                        pltpu.VMEM((n, D_MODEL), F32)],
        compiler_params=pltpu.CompilerParams(vmem_limit_bytes=VMEM_LIMIT),
        name="conv_layer_sample",
    )(x.reshape(n, D_MODEL), state, w_in, b_in, w_dw, b_dw, cg, cb, w_out, b_out, lg, lb)
    return out.reshape(x.shape), ns


def _ffn_kernel(x_ref, wg_ref, wu_ref, wd_ref, g_ref, b_ref, o_ref, acc_ref, xb_ref):
    f = pl.program_id(1)

    @pl.when(f == 0)
    def _():
        acc_ref[...] = jnp.zeros_like(acc_ref)
        xb_ref[...] = x_ref[...].astype(BF16)

    xb = xb_ref[...]
    h = _silu(_dot(xb, wg_ref[...])) * _dot(xb, wu_ref[...])
    acc_ref[...] += _dot(h.astype(BF16), wd_ref[...])

    @pl.when(f == pl.num_programs(1) - 1)
    def _():
        o_ref[...] = _ln(ALPHA * x_ref[...] + acc_ref[...], g_ref[...], b_ref[...])


def _ffn_layer(x, wg, wu, wd, g, b, *, tm=512, n_f=2):
    n = x.shape[0]
    d_ff = wg.shape[1]
    tm = min(tm, n)
    tf = d_ff // n_f
    assert n % tm == 0 and d_ff % n_f == 0 and tf % LANES == 0
    return pl.pallas_call(
        _ffn_kernel,
        grid=(n // tm, n_f),
        in_specs=[
            pl.BlockSpec((tm, D_MODEL), lambda i, f: (i, 0)),
            pl.BlockSpec((D_MODEL, tf), lambda i, f: (0, f)),
            pl.BlockSpec((D_MODEL, tf), lambda i, f: (0, f)),
            pl.BlockSpec((tf, D_MODEL), lambda i, f: (f, 0)),
            _full((1, D_MODEL)), _full((1, D_MODEL)),
        ],
        out_specs=pl.BlockSpec((tm, D_MODEL), lambda i, f: (i, 0)),
        out_shape=jax.ShapeDtypeStruct((n, D_MODEL), F32),
        scratch_shapes=[pltpu.VMEM((tm, D_MODEL), F32), pltpu.VMEM((tm, D_MODEL), BF16)],
        compiler_params=_params(("arbitrary", "arbitrary")),
        name="ffn_dense",
    )(x, wg, wu, wd, g, b)


def _rope_tables(pos):
    half = HEAD_DIM // 2
    inv = 1.0 / (ROPE_THETA ** (jnp.arange(half, dtype=F32) * (2.0 / HEAD_DIM)))
    ang = pos.astype(F32)[:, None] * inv[None, :]
    cos, sin = jnp.cos(ang), jnp.sin(ang)
    return jnp.tile(cos, (1, 4)), jnp.tile(jnp.concatenate([-sin, sin], axis=1), (1, 2))


def _rope(x, cos, sin, first_half):
    half = HEAD_DIM // 2
    out = []
    for c0 in range(0, x.shape[1], LANES):
        slab = x[:, c0:c0 + LANES]
        partner = jnp.where(first_half, pltpu.roll(slab, LANES - half, 1), pltpu.roll(slab, half, 1))
        out.append(slab * cos + partner * sin)
    return jnp.concatenate(out, axis=1)


def _store_dilated(dst_ref, val, s_ref, dil):
    if dil == 1:
        dst_ref[...] = val.astype(BF16)
        return
    d_att = N_HEADS * HEAD_DIM
    rows = val.shape[0] // dil
    for c in range(d_att // LANES):
        s_ref[c] = val[:, c * LANES:(c + 1) * LANES]
    for r in range(dil):
        for c in range(d_att // LANES):
            col = r * d_att + c * LANES
            dst_ref[:, col:col + LANES] = s_ref[c, pl.ds(r, rows, stride=dil), :].astype(BF16)


def _qkv_kernel(x_ref, cos_ref, sin_ref, wkv_ref, wq_ref, kv_ref, *rest, dilated):
    xb = x_ref[...].astype(BF16)
    cos, sin = cos_ref[...], sin_ref[...]
    first_half = (lax.broadcasted_iota(jnp.int32, cos.shape, 1) % HEAD_DIM) < HEAD_DIM // 2
    d_att = N_HEADS * HEAD_DIM
    kv = _dot(xb, wkv_ref[...])
    k = _rope(kv[:, :d_att], cos, sin, first_half)
    v = kv[:, d_att:]
    kv_ref[:, :d_att] = k
    kv_ref[:, d_att:] = v
    q = _rope(_dot(xb, wq_ref[...]), cos, sin, first_half) * (HEAD_DIM ** -0.5)
    if not dilated:
        rest[0][...] = q.astype(BF16)
        return
    s_ref = rest[-1]
    for g, (_, dil) in enumerate(GROUPS):
        q_ref, k_ref, v_ref = rest[3 * g:3 * g + 3]
        _store_dilated(q_ref, q[:, g * d_att:(g + 1) * d_att], s_ref, dil)
        _store_dilated(k_ref, k, s_ref, dil)
        _store_dilated(v_ref, v, s_ref, dil)


def _qkv_layer(x, pos, w_kv, w_q, *, dilated, tm=256):
    n = x.shape[0]
    tm = min(tm, n)
    assert n % tm == 0
    d_att = N_HEADS * HEAD_DIM
    cos, sin = _rope_tables(pos)
    out_specs = [pl.BlockSpec((tm, 2 * d_att), lambda i: (i, 0))]
    out_shape = [jax.ShapeDtypeStruct((n, 2 * d_att), F32)]
    if dilated:
        for _, dil in GROUPS:
            assert tm % (16 * dil) == 0
            out_specs += [pl.BlockSpec((tm // dil, dil * d_att), lambda i: (i, 0))] * 3
            out_shape += [jax.ShapeDtypeStruct((n // dil, dil * d_att), BF16)] * 3
    else:
        out_specs.append(pl.BlockSpec((tm, N_GROUPS * d_att), lambda i: (i, 0)))
        out_shape.append(jax.ShapeDtypeStruct((n, N_GROUPS * d_att), BF16))
    return pl.pallas_call(
        functools.partial(_qkv_kernel, dilated=dilated),
        grid=(n // tm,),
        in_specs=[
            pl.BlockSpec((tm, D_MODEL), lambda i: (i, 0)),
            pl.BlockSpec((tm, LANES), lambda i: (i, 0)),
            pl.BlockSpec((tm, LANES), lambda i: (i, 0)),
            _full((D_MODEL, 2 * d_att)), _full((D_MODEL, N_GROUPS * d_att)),
        ],
        out_specs=out_specs,
        out_shape=out_shape,
        scratch_shapes=[pltpu.VMEM((d_att // LANES, tm, LANES), F32)] if dilated else [],
        compiler_params=_params(("arbitrary",)),
        name="qkv_rope",
    )(x, cos, sin, w_kv, w_q)


def _attn_prompt_kernel(q_ref, kp_ref, kc_ref, vp_ref, vc_ref, o_ref, lse_ref, *, qb):
    blk = pl.program_id(1)
    a = lax.broadcasted_iota(jnp.int32, (qb, 2 * qb), 0)
    c = lax.broadcasted_iota(jnp.int32, (qb, 2 * qb), 1)
    valid = (c >= a) & (c <= a + KEYS_PER_GROUP) & ((blk > 0) | (c >= qb))
    lane = lax.broadcasted_iota(jnp.int32, (qb, LANES), 1)
    low = lane < HEAD_DIM
    lse_all = jnp.zeros((qb, LANES), F32)
    for p in range(N_HEADS // 2):
        sl = slice(p * LANES, (p + 1) * LANES)
        qp = q_ref[:, sl]
        kk = jnp.concatenate([kp_ref[:, sl], kc_ref[:, sl]], axis=0)
        vv = jnp.concatenate([vp_ref[:, sl], vc_ref[:, sl]], axis=0)
        outs = []
        for j in range(2):
            qm = jnp.where(low if j == 0 else jnp.logical_not(low), qp, jnp.zeros_like(qp))
            s = lax.dot_general(qm, kk, (((1,), (1,)), ((), ())), preferred_element_type=F32)
            s = jnp.where(valid, s, NEG)
            m = jnp.max(s, axis=1, keepdims=True)
            e = jnp.exp(s - m)
            den = jnp.sum(e, axis=1, keepdims=True)
            outs.append(_dot(e.astype(BF16), vv) / den)
            lse_all = jnp.where(lane == 2 * p + j, m + jnp.log(den), lse_all)
        o_ref[:, sl] = jnp.where(low, outs[0], outs[1]).astype(BF16)
    lse_ref[0] = lse_all


def _attn_prompt_group(qg, kg, vg, g, *, qb=128):
    dil = GROUPS[g][1]
    assert GROUPS[g][0] // dil == KEYS_PER_GROUP and qb == KEYS_PER_GROUP
    d_att = N_HEADS * HEAD_DIM
    rows = kg.shape[0]
    assert kg.shape[1] == dil * d_att and rows % qb == 0
    cur = pl.BlockSpec((qb, d_att), lambda r, b: (b, r))
    prev = pl.BlockSpec((qb, d_att), lambda r, b: (jnp.maximum(b - 1, 0), r))
    return pl.pallas_call(
        functools.partial(_attn_prompt_kernel, qb=qb),
        grid=(dil, rows // qb),
        in_specs=[cur, prev, cur, prev, cur],
        out_specs=[cur, pl.BlockSpec((1, qb, LANES), lambda r, b: (r, b, 0))],
        out_shape=[jax.ShapeDtypeStruct((rows, dil * d_att), BF16), jax.ShapeDtypeStruct((dil, rows, LANES), F32)],
        compiler_params=_params(("arbitrary", "arbitrary")),
        name=f"attn_prompt_g{g}",
    )(qg, kg, kg, vg, vg)


def _sample_key_plan(t_len, past):
    d_max = max(d for _, d in GROUPS)
    n_recent = max(w for w, d in GROUPS if d < d_max)
    assert past % d_max == 0 and t_len <= d_max and n_recent <= past and (PAST_LEN - past) % d_max == 0
    n_strided = past // d_max * t_len
    n_keys = n_strided + n_recent + LANES
    row_of_key = np.full((n_keys,), -1, np.int64)
    row_of_key[:n_strided] = (np.arange(n_strided) // t_len) * d_max + np.arange(n_strided) % t_len
    row_of_key[n_strided:n_strided + n_recent] = past - n_recent + np.arange(n_recent)
    row_of_key[n_strided + n_recent:n_strided + n_recent + t_len] = past + np.arange(t_len)
    part = np.zeros((n_keys,), np.int64)
    part[n_strided:] = 1
    part[n_strided + n_recent:] = 2
    part[n_strided + n_recent + t_len:] = 3
    n_rows = 16
    valid = np.zeros((n_keys, n_rows * N_HEADS), np.float32)
    for r in range(N_GROUPS * t_len):
        g, i = divmod(r, t_len)
        window, dil = GROUPS[g]
        dist = past + i - row_of_key
        ok = (dist >= 0) & (dist <= window) & (dist % dil == 0)
        ok &= np.where(dil == d_max, part != 1, part != 0) & (part != 3)
        assert ok.sum() == window // dil + 1
        valid[:, r * N_HEADS:(r + 1) * N_HEADS] = ok[:, None]
    return n_strided, n_recent, n_keys, valid


def _attn_sample_kernel(q_ref, kvn_ref, valid_ref, c_ref, o_ref, kb_ref, vb_ref, a0_ref, r0_ref, a1_ref,
                        r1_ref, sem, *, t_len, past, n_strided, n_recent, n_keys):
    step, n_steps = pl.program_id(0), pl.num_programs(0)
    slots = ((a0_ref, r0_ref), (a1_ref, r1_ref))
    per_key = 2 * N_HEADS
    d_max = max(d for _, d in GROUPS)
    run = t_len * per_key

    def recent_copy(b, slot):
        src = c_ref.at[b, pl.ds((past - n_recent) * per_key, n_recent * per_key), :]
        return pltpu.make_async_copy(src, slots[slot][1], sem.at[1, slot])

    def strided_copy(b, slot, a):
        src = c_ref.at[b, pl.ds(pl.multiple_of(a * (d_max * per_key), d_max * per_key), run), :]
        return pltpu.make_async_copy(src, slots[slot][0].at[pl.ds(pl.multiple_of(a * run, run), run), :],
                                     sem.at[0, slot])

    def start(b, slot):
        recent_copy(b, slot).start()
        lax.fori_loop(0, past // d_max, lambda a, c: (strided_copy(b, slot, a).start(), c)[1], 0)

    def wait(b, slot):
        recent_copy(b, slot).wait()
        lax.fori_loop(0, past // d_max, lambda a, c: (strided_copy(b, slot, a).wait(), c)[1], 0)

    @pl.when(step == 0)
    def _():
        start(0, 0)

    start(2 * step + 1, 1)
    wait(2 * step, 0)
    _attn_sample_one(q_ref, kvn_ref, valid_ref, o_ref, kb_ref, vb_ref, *slots[0], 0, t_len=t_len,
                     n_strided=n_strided, n_recent=n_recent, n_keys=n_keys)

    @pl.when(step + 1 < n_steps)
    def _():
        start(2 * step + 2, 0)

    wait(2 * step + 1, 1)
    _attn_sample_one(q_ref, kvn_ref, valid_ref, o_ref, kb_ref, vb_ref, *slots[1], 1, t_len=t_len,
                     n_strided=n_strided, n_recent=n_recent, n_keys=n_keys)


def _attn_sample_one(q_ref, kvn_ref, valid_ref, o_ref, kb_ref, vb_ref, strided_ref, recent_ref, which, *, t_len,
                     n_strided, n_recent, n_keys):
    d_att = N_HEADS * HEAD_DIM
    n_rows = 16
    n_cols = N_HEADS * n_rows
    per_key = 2 * N_HEADS
    for buf_ref, kv in ((kb_ref, 0), (vb_ref, 1)):
        for p in range(N_HEADS // 2):
            for src, n_src, off in ((strided_ref, n_strided, 0), (recent_ref, n_recent, n_strided)):
                pair = [src[pl.ds(kv * N_HEADS + 2 * p + j, n_src, stride=per_key), :] for j in range(2)]
                buf_ref[off:off + n_src, p * LANES:(p + 1) * LANES] = jnp.concatenate(pair, axis=1).astype(BF16)
    n_old = n_strided + n_recent
    kn = jnp.concatenate([kvn_ref[which], jnp.zeros((n_keys - n_old - t_len, 2 * d_att), F32)], axis=0)
    kb_ref[n_old:n_keys, :] = kn[:, :d_att].astype(BF16)
    vb_ref[n_old:n_keys, :] = kn[:, d_att:].astype(BF16)
    q = q_ref[which].astype(F32)
    row = lax.broadcasted_iota(jnp.int32, (N_HEADS, d_att), 0)
    lane = lax.broadcasted_iota(jnp.int32, (N_HEADS, d_att), 1)
    own = row == lane // HEAD_DIM
    blocks = []
    for r in range(n_rows):
        g, i = divmod(r, t_len)
        if g < N_GROUPS:
            qr = q[i:i + 1, g * d_att:(g + 1) * d_att]
            blocks.append(jnp.where(own, jnp.broadcast_to(qr, (N_HEADS, d_att)), 0.0))
        else:
            blocks.append(jnp.zeros((N_HEADS, d_att), F32))
    qbd = jnp.concatenate(blocks, axis=0).astype(BF16)
    s = lax.dot_general(kb_ref[...], qbd, (((1,), (1,)), ((), ())), preferred_element_type=F32)
    s = jnp.where(valid_ref[...] > 0.0, s, NEG)
    m = jnp.max(s, axis=0, keepdims=True)
    quarter = n_cols // 4
    m = jnp.maximum(jnp.maximum(m, pltpu.roll(m, quarter, 1)),
                    jnp.maximum(pltpu.roll(m, 2 * quarter, 1), pltpu.roll(m, 3 * quarter, 1)))
    e = jnp.exp(s - m)
    den = jnp.sum(e, axis=0, keepdims=True)
    den = den + pltpu.roll(den, quarter, 1) + pltpu.roll(den, 2 * quarter, 1) + pltpu.roll(den, 3 * quarter, 1)
    p = (e / den).T.astype(BF16)
    o = _dot(p, vb_ref[...]).reshape(n_rows, N_HEADS, d_att)
    o = jnp.sum(jnp.where(own[None], o, 0.0), axis=1)
    out = o[0:t_len]
    for g in range(1, N_GROUPS):
        out = out + o[g * t_len:(g + 1) * t_len]
    o_ref[which] = out


def _attn_sample(q, kv_new, cache):
    bsz, t_len, _ = q.shape
    past = cache.shape[1]
    d_att = N_HEADS * HEAD_DIM
    d_max = max(d for _, d in GROUPS)
    assert N_GROUPS * t_len <= 16 and bsz % 2 == 0
    assert PAST_LEN >= past and past >= max(w for w, _ in GROUPS)
    n_strided, n_recent, n_keys, valid = _sample_key_plan(t_len, past)
    per_key = 2 * N_HEADS
    cache2 = cache.reshape(bsz, past * per_key, HEAD_DIM)
    strided_buf = pltpu.VMEM((n_strided * per_key, HEAD_DIM), F32)
    recent_buf = pltpu.VMEM((n_recent * per_key, HEAD_DIM), F32)
    return pl.pallas_call(
        functools.partial(_attn_sample_kernel, t_len=t_len, past=past, n_strided=n_strided, n_recent=n_recent,
                          n_keys=n_keys),
        grid=(bsz // 2,),
        in_specs=[
            pl.BlockSpec((2, t_len, N_GROUPS * d_att), lambda b: (b, 0, 0)),
            pl.BlockSpec((2, t_len, 2 * d_att), lambda b: (b, 0, 0)),
            _full(valid.shape),
            pl.BlockSpec(memory_space=pl.ANY),
        ],
        out_specs=pl.BlockSpec((2, t_len, d_att), lambda b: (b, 0, 0)),
        out_shape=jax.ShapeDtypeStruct((bsz, t_len, d_att), F32),
        scratch_shapes=[pltpu.VMEM((n_keys, d_att), BF16), pltpu.VMEM((n_keys, d_att), BF16),
                        strided_buf, recent_buf, strided_buf, recent_buf, pltpu.SemaphoreType.DMA((2, 2))],
        compiler_params=_params(("arbitrary",)),
        name="attn_sample",
    )(q, kv_new, jnp.asarray(valid), cache2)


def _store_token_tiles(ref, x):
    n = x.shape[0]
    for s in range(TILE_ROWS):
        ref[pl.ds(s, n, stride=TILE_ROWS), :] = x[:, s * LANES:(s + 1) * LANES]


def _load_token_slab(ref, n, s):
    return ref[pl.ds(s, n, stride=TILE_ROWS), :]


def _token_rows(ref, t):
    return ref.at[pl.ds(pl.multiple_of(t * TILE_ROWS, TILE_ROWS), TILE_ROWS), :]


def _split3(w):
    hi = w.astype(BF16)
    r1 = w - hi.astype(F32)
    mid = r1.astype(BF16)
    lo = (r1 - mid.astype(F32)).astype(BF16)
    return hi, mid, lo


def _oproj_router_tail(att, x_ref, wo_ref, g_ref, b_ref, wrh_ref, wrl_ref, x3t_ref, route_ref, cnt_ref):
    tm = att.shape[0]

    @pl.when(pl.program_id(0) == 0)
    def _():
        cnt_ref[...] = jnp.zeros_like(cnt_ref)

    x3 = _ln(ALPHA * x_ref[...] + _dot(att.astype(BF16), wo_ref[...]), g_ref[...], b_ref[...])
    _store_token_tiles(x3t_ref, x3)
    xh = x3.astype(BF16)
    xl = (x3 - xh.astype(F32)).astype(BF16)
    logits = _dot(xh, wrh_ref[...]) + (_dot(xh, wrl_ref[...]) + _dot(xl, wrh_ref[...]))
    lane = lax.broadcasted_iota(jnp.int32, logits.shape, 1)
    logits = jnp.where(lane < N_EXPERTS, logits, NEG)
    v1 = jnp.max(logits, axis=1, keepdims=True)
    i1 = jnp.min(jnp.where(logits == v1, lane, LANES), axis=1, keepdims=True)
    rest = jnp.where(lane == i1, NEG, logits)
    v2 = jnp.max(rest, axis=1, keepdims=True)
    i2 = jnp.min(jnp.where(rest == v2, lane, LANES), axis=1, keepdims=True)
    e2 = jnp.exp(v2 - v1)
    den = 1.0 + e2
    sel = jnp.where((lane == i1) | (lane == i2), 1.0, 0.0)
    tri = jnp.where(lax.broadcasted_iota(jnp.int32, (tm, tm), 1) < lax.broadcasted_iota(jnp.int32, (tm, tm), 0),
                    1.0, 0.0).astype(BF16)
    before = _dot(tri, sel.astype(BF16)) + cnt_ref[0:1, :]
    rank1 = jnp.sum(jnp.where(lane == i1, before, 0.0), axis=1, keepdims=True)
    rank2 = jnp.sum(jnp.where(lane == i2, before, 0.0), axis=1, keepdims=True)
    cnt_ref[0:1, :] = cnt_ref[0:1, :] + jnp.sum(sel, axis=0, keepdims=True)
    fields = (i1.astype(F32), i2.astype(F32), 1.0 / den, e2 / den, rank1, rank2)
    route = jnp.zeros(logits.shape, F32)
    for k, val in enumerate(fields):
        route = jnp.where(lane == k, val, route)
    route_ref[...] = route


def _load_dilated(o_ref, lse_ref, o_scr, l_scr, dil):
    if dil == 1:
        return o_ref[...].astype(F32), lse_ref[0]
    d_att = N_HEADS * HEAD_DIM
    rows = o_ref.shape[0]
    n_slabs = d_att // LANES
    for r in range(dil):
        for c in range(n_slabs):
            col = r * d_att + c * LANES
            o_scr[c, pl.ds(r, rows, stride=dil), :] = o_ref[:, col:col + LANES].astype(F32)
        l_scr[pl.ds(r, rows, stride=dil), :] = lse_ref[r]
    return jnp.concatenate([o_scr[c] for c in range(n_slabs)], axis=1), l_scr[...]


def _oproj_prompt_kernel(o0_ref, l0_ref, o1_ref, l1_ref, o2_ref, l2_ref, ex_ref, x_ref, wo_ref, g_ref, b_ref,
                         wrh_ref, wrl_ref, x3t_ref, route_ref, cnt_ref, *scr):
    parts = [_load_dilated(o_ref, l_ref, scr[2 * g], scr[2 * g + 1], GROUPS[g][1])
             for g, (o_ref, l_ref) in enumerate(((o0_ref, l0_ref), (o1_ref, l1_ref), (o2_ref, l2_ref)))]
    lse = [l for _, l in parts]
    m = jnp.maximum(jnp.maximum(lse[0], lse[1]), lse[2])
    w = [jnp.exp(l - m) for l in lse]
    den = w[0] + w[1] + w[2]
    att = None
    for g, (o, _) in enumerate(parts):
        hi, mid, lo = _split3(w[g] / den)
        wexp = _dot(hi, ex_ref[...]) + (_dot(mid, ex_ref[...]) + _dot(lo, ex_ref[...]))
        term = wexp * o
        att = term if att is None else att + term
    _oproj_router_tail(att, x_ref, wo_ref, g_ref, b_ref, wrh_ref, wrl_ref, x3t_ref, route_ref, cnt_ref)


def _oproj_sample_kernel(att_ref, x_ref, wo_ref, g_ref, b_ref, wrh_ref, wrl_ref, x3t_ref, route_ref, cnt_ref):
    _oproj_router_tail(att_ref[...], x_ref, wo_ref, g_ref, b_ref, wrh_ref, wrl_ref, x3t_ref, route_ref, cnt_ref)


def _router_split(w_router):
    wr = jnp.zeros((D_MODEL, LANES), F32).at[:, :N_EXPERTS].set(w_router)
    hi = wr.astype(BF16)
    return hi, (wr - hi.astype(F32)).astype(BF16)


def _oproj_router(att_parts, x, w_o, g, b, w_router, *, tm=256):
    n = x.shape[0]
    tm = min(tm, n)
    assert n % tm == 0
    d_att = N_HEADS * HEAD_DIM
    wrh, wrl = _router_split(w_router)
    row_blk = lambda w: pl.BlockSpec((tm, w), lambda i: (i, 0))
    tail_specs = [row_blk(D_MODEL), _full((d_att, D_MODEL)), _full((1, D_MODEL)), _full((1, D_MODEL)),
                  _full((D_MODEL, LANES)), _full((D_MODEL, LANES))]
    tail_args = (x, w_o, g, b, wrh, wrl)
    scratch = []
    if len(att_parts) == 1:
        kern, specs, args = _oproj_sample_kernel, [row_blk(d_att)], tuple(att_parts)
    else:
        expand = (np.arange(LANES)[:, None] == np.arange(d_att)[None, :] // HEAD_DIM).astype(np.float32)
        kern, specs, args = _oproj_prompt_kernel, [], []
        for (_, dil), (o, lse) in zip(GROUPS, att_parts):
            specs += [pl.BlockSpec((tm // dil, dil * d_att), lambda i: (i, 0)),
                      pl.BlockSpec((dil, tm // dil, LANES), lambda i: (0, i, 0))]
            args += [o, lse]
            scratch += [pltpu.VMEM((d_att // LANES, tm, LANES), F32), pltpu.VMEM((tm, LANES), F32)]
        specs.append(_full((LANES, d_att)))
        args.append(jnp.asarray(expand, BF16))
    return pl.pallas_call(
        kern,
        grid=(n // tm,),
        in_specs=specs + tail_specs,
        out_specs=[pl.BlockSpec((tm * TILE_ROWS, LANES), lambda i: (i, 0)), row_blk(LANES), _full((TILE_ROWS, LANES))],
        out_shape=[jax.ShapeDtypeStruct((n * TILE_ROWS, LANES), F32), jax.ShapeDtypeStruct((n, LANES), F32),
                   jax.ShapeDtypeStruct((TILE_ROWS, LANES), F32)],
        scratch_shapes=scratch,
        compiler_params=_params(("arbitrary",)),
        name="oproj_router",
    )(*args, *tail_args)


def _route_plan(routes, counts, tile, n_tiles):
    cnt = [c[0, :N_EXPERTS].astype(jnp.int32) for c in counts]
    total = sum(cnt)
    padded = (total + tile - 1) // tile * tile
    ends = jnp.cumsum(padded)
    offs = ends - padded
    fill = jnp.concatenate([offs + total, ends]).astype(jnp.int32)
    pos = []
    for route, c in zip(routes, cnt):
        e = route[:, 0:2].astype(jnp.int32)
        pos.append((offs[e] + route[:, 4:6].astype(jnp.int32)).reshape(-1))
        offs = offs + c
    n_live = (ends[-1] // tile).astype(jnp.int32)
    tile_id = jnp.minimum(jnp.arange(n_tiles, dtype=jnp.int32), n_live - 1)
    tile_expert = jnp.sum((ends // tile)[None, :] <= tile_id[:, None], axis=1).astype(jnp.int32)
    return pos, tile_expert, n_live.reshape(1), fill


def _dispatch_kernel(pos_ref, fill_ref, xa_ref, xb_ref, xs_ref, z_ref, sem, *, tm, steps_a, n_b, tile, n_tiles):
    i = pl.program_id(0)

    def scatter(src_ref, count, first_token):
        def row_copy(r, s):
            dst = _token_rows(xs_ref, pos_ref[(first_token + r) * 2 + s])
            return pltpu.make_async_copy(_token_rows(src_ref, r), dst, sem)

        def issue(r, c):
            row_copy(r, 0).start(priority=0)
            row_copy(r, 1).start(priority=1)
            return c

        def drain(r, c):
            row_copy(r, 0).wait()
            row_copy(r, 1).wait()
            return c

        lax.fori_loop(0, count, issue, 0)
        lax.fori_loop(0, count, drain, 0)

    @pl.when(i < steps_a)
    def _():
        scatter(xa_ref, tm, i * tm)

    @pl.when(i == steps_a)
    def _():
        scatter(xb_ref, n_b, steps_a * tm)
        z_ref[...] = jnp.zeros_like(z_ref)
        zero_row = lambda r: pltpu.make_async_copy(z_ref.at[0:TILE_ROWS, :], _token_rows(xs_ref, r), sem)
        tile_rows = tile * TILE_ROWS
        zero_tile = lambda j: pltpu.make_async_copy(
            z_ref, xs_ref.at[pl.ds(pl.multiple_of(j * tile_rows, tile_rows), tile_rows), :], sem)

        def loop(lo, hi, copy):
            def issue(r, c):
                copy(r).start()
                return c

            def drain(r, c):
                copy(r).wait()
                return c

            lax.fori_loop(lo, hi, issue, 0)
            lax.fori_loop(lo, hi, drain, 0)

        for e in range(N_EXPERTS):
            loop(fill_ref[e], fill_ref[N_EXPERTS + e], zero_row)
        loop(fill_ref[2 * N_EXPERTS - 1] // tile, n_tiles, zero_tile)


def _dispatch(pos, fill_rng, x3t_a, x3t_b, n_tiles, tile, *, tm=256):
    n_a, n_b = x3t_a.shape[0] // TILE_ROWS, x3t_b.shape[0] // TILE_ROWS
    tm = min(tm, n_a)
    assert n_a % tm == 0
    steps_a = n_a // tm
    return pl.pallas_call(
        functools.partial(_dispatch_kernel, tm=tm, steps_a=steps_a, n_b=n_b, tile=tile, n_tiles=n_tiles),
        grid_spec=pltpu.PrefetchScalarGridSpec(
            num_scalar_prefetch=2, grid=(steps_a + 1,),
            in_specs=[pl.BlockSpec((tm * TILE_ROWS, LANES), lambda i, *_: (jnp.minimum(i, steps_a - 1), 0)),
                      pl.BlockSpec((n_b * TILE_ROWS, LANES), lambda i, *_: (0, 0))],
            out_specs=pl.BlockSpec(memory_space=pl.ANY),
            scratch_shapes=[pltpu.VMEM((tile * TILE_ROWS, LANES), F32), pltpu.SemaphoreType.DMA(())]),
        out_shape=jax.ShapeDtypeStruct((n_tiles * tile * TILE_ROWS, LANES), F32),
        compiler_params=_params(("arbitrary",)),
        name="moe_dispatch",
    )(pos, fill_rng, x3t_a, x3t_b)


def _moe_ffn_kernel(te_ref, nl_ref, xs_ref, wg_ref, wu_ref, wd_ref, ys_ref, acc_ref, xb_ref, *, tile):
    j, f = pl.program_id(0), pl.program_id(1)

    @pl.when(j < nl_ref[0])
    def _():
        @pl.when(f == 0)
        def _():
            for s in range(TILE_ROWS):
                xb_ref[:, s * LANES:(s + 1) * LANES] = _load_token_slab(xs_ref, tile, s).astype(BF16)

        xb = xb_ref[...]
        h = _silu(_dot(xb, wg_ref[...])) * _dot(xb, wu_ref[...])
        y = _dot(h.astype(BF16), wd_ref[...])

        @pl.when(f == 0)
        def _():
            acc_ref[...] = y

        @pl.when(f > 0)
        def _():
            acc_ref[...] += y

        @pl.when(f == pl.num_programs(1) - 1)
        def _():
            _store_token_tiles(ys_ref, acc_ref[...])

    @pl.when((j >= nl_ref[0]) & (f == pl.num_programs(1) - 1))
    def _():
        ys_ref[...] = jnp.zeros_like(ys_ref)


def _moe_ffn(xs, tile_expert, n_live, wg, wu, wd, *, tile, n_f=4):
    n_tiles = tile_expert.shape[0]
    d_ff = wg.shape[2]
    tf = d_ff // n_f
    assert d_ff % n_f == 0 and tf % LANES == 0 and xs.shape[0] == n_tiles * tile * TILE_ROWS
    row_map = lambda j, f, te, nl: (jnp.minimum(j, nl[0] - 1), 0)
    f_idx = lambda j, f, nl: jnp.where(j < nl[0], f, n_f - 1)
    return pl.pallas_call(
        functools.partial(_moe_ffn_kernel, tile=tile),
        grid_spec=pltpu.PrefetchScalarGridSpec(
            num_scalar_prefetch=2, grid=(n_tiles, n_f),
            in_specs=[
                pl.BlockSpec((tile * TILE_ROWS, LANES), row_map),
                pl.BlockSpec((None, D_MODEL, tf), lambda j, f, te, nl: (te[j], 0, f_idx(j, f, nl))),
                pl.BlockSpec((None, D_MODEL, tf), lambda j, f, te, nl: (te[j], 0, f_idx(j, f, nl))),
                pl.BlockSpec((None, tf, D_MODEL), lambda j, f, te, nl: (te[j], f_idx(j, f, nl), 0)),
            ],
            out_specs=pl.BlockSpec((tile * TILE_ROWS, LANES), lambda j, f, te, nl: (j, 0)),
            scratch_shapes=[pltpu.VMEM((tile, D_MODEL), F32), pltpu.VMEM((tile, D_MODEL), BF16)]),
        out_shape=jax.ShapeDtypeStruct(xs.shape, F32),
        compiler_params=_params(("arbitrary", "arbitrary")),
        name="moe_ffn",
    )(tile_expert, n_live, xs, wg, wu, wd)


def _moe_combine_kernel(pos_ref, route_ref, x3t_ref, ys_ref, g_ref, b_ref, o_ref, y1_ref, y2_ref, pre_ref, sem, *, tm):
    i = pl.program_id(0)
    bufs = (y1_ref, y2_ref)

    def row_copy(r, s):
        return pltpu.make_async_copy(_token_rows(ys_ref, pos_ref[(i * tm + r) * 2 + s]), _token_rows(bufs[s], r), sem)

    def issue(r, c):
        row_copy(r, 0).start(priority=0)
        row_copy(r, 1).start(priority=1)
        return c

    def drain(r, c):
        row_copy(r, 0).wait()
        row_copy(r, 1).wait()
        return c

    lax.fori_loop(0, tm, issue, 0)
    lax.fori_loop(0, tm, drain, 0)
    route = route_ref[...]
    g1, g2 = route[:, 2:3], route[:, 3:4]
    for s in range(TILE_ROWS):
        y = g1 * _load_token_slab(y1_ref, tm, s) + g2 * _load_token_slab(y2_ref, tm, s)
        pre_ref[:, s * LANES:(s + 1) * LANES] = ALPHA * _load_token_slab(x3t_ref, tm, s) + y
    o_ref[...] = _ln(pre_ref[...], g_ref[...], b_ref[...])


def _moe_combine(pos, route, x3t, ys, g, b, *, tm=256):
    n = route.shape[0]
    tm = min(tm, n)
    assert n % tm == 0
    tok_buf = pltpu.VMEM((tm * TILE_ROWS, LANES), F32)
    return pl.pallas_call(
        functools.partial(_moe_combine_kernel, tm=tm),
        grid_spec=pltpu.PrefetchScalarGridSpec(
            num_scalar_prefetch=1, grid=(n // tm,),
            in_specs=[
                pl.BlockSpec((tm, LANES), lambda i, *_: (i, 0)),
                pl.BlockSpec((tm * TILE_ROWS, LANES), lambda i, *_: (i, 0)),
                pl.BlockSpec(memory_space=pl.ANY),
                pl.BlockSpec((1, D_MODEL), lambda i, *_: (0, 0)),
                pl.BlockSpec((1, D_MODEL), lambda i, *_: (0, 0)),
            ],
            out_specs=pl.BlockSpec((tm, D_MODEL), lambda i, *_: (i, 0)),
            scratch_shapes=[tok_buf, tok_buf, pltpu.VMEM((tm, D_MODEL), F32), pltpu.SemaphoreType.DMA(())]),
        out_shape=jax.ShapeDtypeStruct((n, D_MODEL), F32),
        compiler_params=_params(("arbitrary",)),
        name="moe_combine",
    )(pos, route, x3t, ys, g, b)


def kernel(x_prompt, x_sample, cache_kv, state_conv, conv_w_in, conv_b_in, conv_w_dw, conv_b_dw, conv_ln_g,
           conv_ln_b, conv_w_out, conv_b_out, ffn_w_gate, ffn_w_up, ffn_w_down, w_kv, attn_w_q, attn_w_o,
           moe_w_router, moe_w_gate, moe_w_up, moe_w_down, ln_g, ln_b):
    bp, t_p, _ = x_prompt.shape
    bs, t_s, _ = x_sample.shape
    assert bp == 1 and conv_w_in.shape[0] == 1 and attn_w_q.shape[0] == 1
    d_att = N_HEADS * HEAD_DIM
    row = lambda a: a.reshape(1, -1)
    bf = lambda a: a.astype(BF16)

    conv_w = (bf(conv_w_in[0]), row(conv_b_in[0]), conv_w_dw[0], row(conv_b_dw[0]), row(conv_ln_g[0]),
              row(conv_ln_b[0]), bf(conv_w_out[0]), row(conv_b_out[0]), row(ln_g[0, 0]), row(ln_b[0, 0]))
    ffn_w = (bf(ffn_w_gate[0]), bf(ffn_w_up[0]), bf(ffn_w_down[0]), row(ln_g[0, 1]), row(ln_b[0, 1]))
    wkv_b, wq_b, wo_b = bf(w_kv), bf(attn_w_q[0]), bf(attn_w_o[0])
    ln10 = (row(ln_g[1, 0]), row(ln_b[1, 0]))
    ln11 = (row(ln_g[1, 1]), row(ln_b[1, 1]))

    x1, conv_p = _conv_layer_prompt(x_prompt, jnp.zeros((bp, HALO, D_MODEL), F32), *conv_w)
    x2 = _ffn_layer(x1.reshape(t_p, D_MODEL), *ffn_w)
    kv_p, *qkv_views = _qkv_layer(x2, jnp.arange(t_p, dtype=jnp.int32), wkv_b, wq_b, dilated=True)
    parts = [_attn_prompt_group(*qkv_views[3 * g:3 * g + 3], g) for g in range(N_GROUPS)]
    x3t_p, route_p, cnt_p = _oproj_router(parts, x2, wo_b, *ln10, moe_w_router[0])
    n_keep = min(max(w for w, _ in GROUPS), t_p)
    kv_prompt = kv_p[t_p - n_keep:].reshape(bp, n_keep, 2, N_HEADS, HEAD_DIM)

    n_s = bs * t_s
    x1s, conv_s = _conv_layer_sample(x_sample, state_conv[0], *conv_w)
    x2s = _ffn_layer(x1s.reshape(n_s, D_MODEL), *ffn_w)
    pos_s = PAST_LEN + jnp.tile(jnp.arange(t_s, dtype=jnp.int32), bs)
    kv_s, q_s = _qkv_layer(x2s, pos_s, wkv_b, wq_b, dilated=False)
    att_s = _attn_sample(q_s.reshape(bs, t_s, N_GROUPS * d_att), kv_s.reshape(bs, t_s, 2 * d_att), cache_kv)
    x3t_s, route_s, cnt_s = _oproj_router((att_s.reshape(n_s, d_att),), x2s, wo_b, *ln10, moe_w_router[0])
    kv_sample = kv_s.reshape(bs, t_s, 2, N_HEADS, HEAD_DIM)

    n_slots = 2 * (t_p + n_s)
    n_tiles = (n_slots + N_EXPERTS * (MOE_TILE - 1)) // MOE_TILE
    (pos_p, pos_sm), tile_expert, n_live, fill_rng = _route_plan((route_p, route_s), (cnt_p, cnt_s), MOE_TILE, n_tiles)
    xs = _dispatch(jnp.concatenate([pos_p, pos_sm]), fill_rng, x3t_p, x3t_s, n_tiles, MOE_TILE)
    ys = _moe_ffn(xs, tile_expert, n_live, bf(moe_w_gate[0]), bf(moe_w_up[0]), bf(moe_w_down[0]), tile=MOE_TILE)
    y_prompt = _moe_combine(pos_p, route_p, x3t_p, ys, *ln11).reshape(bp, t_p, D_MODEL)
    y_sample = _moe_combine(pos_sm, route_s, x3t_s, ys, *ln11).reshape(bs, t_s, D_MODEL)

    return (y_prompt, y_sample, conv_p[None], conv_s[None], kv_prompt, kv_sample)
```

```python
import functools

import numpy as np
import jax
import jax.numpy as jnp
from jax import lax
from jax.experimental import pallas as pl
from jax.experimental.pallas import tpu as pltpu

F32 = jnp.float32
BF16 = jnp.bfloat16

D_MODEL = 1024
N_HEADS = 16
HEAD_DIM = 64
GROUPS = ((128, 1), (512, 4), (2048, 16))
N_GROUPS = len(GROUPS)
KEYS_PER_GROUP = 128
CONV_WIDTH = 31
HALO = CONV_WIDTH - 1
N_EXPERTS = 8
ROPE_THETA = 10000.0
LN_EPS = 1e-5
DEPTH = 2
ALPHA = (2 * DEPTH) ** 0.25
PAST_LEN = 16384
NEG = -1e30

LANES = 128
HALO_PAD = 32
TILE_ROWS = 8
MOE_TILE = 512
VMEM_LIMIT = 56 * 1024 * 1024


def _params(sem, vmem=VMEM_LIMIT):
    return pltpu.CompilerParams(dimension_semantics=sem, vmem_limit_bytes=vmem)


def _dot(a, b):
    return jnp.dot(a, b, preferred_element_type=F32)


def _sigmoid(x):
    return 1.0 / (1.0 + jnp.exp(-x))


def _silu(x):
    return x * _sigmoid(x)


def _ln(x, g, b):
    xc = x - jnp.mean(x, axis=-1, keepdims=True)
    var = jnp.mean(xc * xc, axis=-1, keepdims=True)
    return xc * lax.rsqrt(var + LN_EPS) * g + b


def _full(shape):
    return pl.BlockSpec(shape, lambda *_: (0,) * len(shape))


def _conv_taps(s_ref, wdw_ref, bdw_ref, c_ref, n_rows, row_chunk):
    base = HALO_PAD - HALO

    def chunk(rc, carry):
        r0 = pl.multiple_of(rc * row_chunk, row_chunk)
        win = s_ref.at[pl.ds(r0, row_chunk + HALO_PAD), :]
        for c0 in range(0, D_MODEL, LANES):
            acc = jnp.broadcast_to(bdw_ref[:, c0:c0 + LANES], (row_chunk, LANES))
            for k in range(CONV_WIDTH):
                acc = acc + wdw_ref[k:k + 1, c0:c0 + LANES] * win[base + k:base + k + row_chunk, c0:c0 + LANES]
            c_ref[pl.ds(r0, row_chunk), c0:c0 + LANES] = acc
        return carry

    lax.fori_loop(0, n_rows // row_chunk, chunk, 0)


def _conv_prompt_kernel(x_ref, st_ref, win_ref, bin_ref, wdw_ref, bdw_ref, cg_ref, cb_ref, wout_ref, bout_ref,
                        lg_ref, lb_ref, o_ref, ns_ref, s_ref, c_ref, *, tm, row_chunk):
    base = HALO_PAD - HALO

    @pl.when(pl.program_id(1) == 0)
    def _():
        s_ref[0:HALO_PAD, :] = jnp.zeros((HALO_PAD, D_MODEL), F32)
        s_ref[base:HALO_PAD, :] = st_ref[0]

    x = x_ref[0]
    h = _dot(x.astype(BF16), win_ref[...]) + bin_ref[...]
    s_ref[HALO_PAD:HALO_PAD + tm, :] = h[:, :D_MODEL] * _sigmoid(h[:, D_MODEL:])
    _conv_taps(s_ref, wdw_ref, bdw_ref, c_ref, tm, row_chunk)
    c = _silu(_ln(c_ref[...], cg_ref[...], cb_ref[...]))
    mix = _dot(c.astype(BF16), wout_ref[...]) + bout_ref[...]
    o_ref[0] = _ln(ALPHA * x + mix, lg_ref[...], lb_ref[...])
    tail = s_ref[pl.ds(tm + base, HALO), :]
    s_ref[base:HALO_PAD, :] = tail
    ns_ref[0] = tail


def _conv_layer_prompt(x, state, w_in, b_in, w_dw, b_dw, cg, cb, w_out, b_out, lg, lb, *, tm=256):
    bsz, t_len, _ = x.shape
    tm = min(tm, t_len)
    assert t_len % tm == 0 and tm >= HALO_PAD and tm % 8 == 0
    row_chunk = 64 if tm % 64 == 0 else 8
    row = lambda n: _full((1, n))
    return pl.pallas_call(
        functools.partial(_conv_prompt_kernel, tm=tm, row_chunk=row_chunk),
        grid=(bsz, t_len // tm),
        in_specs=[
            pl.BlockSpec((1, tm, D_MODEL), lambda b, i: (b, i, 0)),
            pl.BlockSpec((1, HALO, D_MODEL), lambda b, i: (b, 0, 0)),
            _full((D_MODEL, 2 * D_MODEL)), row(2 * D_MODEL),
            _full((CONV_WIDTH, D_MODEL)), row(D_MODEL), row(D_MODEL), row(D_MODEL),
            _full((D_MODEL, D_MODEL)), row(D_MODEL), row(D_MODEL), row(D_MODEL),
        ],
        out_specs=[
            pl.BlockSpec((1, tm, D_MODEL), lambda b, i: (b, i, 0)),
            pl.BlockSpec((1, HALO, D_MODEL), lambda b, i: (b, 0, 0)),
        ],
        out_shape=[jax.ShapeDtypeStruct(x.shape, F32), jax.ShapeDtypeStruct((bsz, HALO, D_MODEL), F32)],
        scratch_shapes=[pltpu.VMEM((HALO_PAD + tm, D_MODEL), F32), pltpu.VMEM((tm, D_MODEL), F32)],
        compiler_params=_params(("arbitrary", "arbitrary")),
        name="conv_layer_prompt",
    )(x, state, w_in, b_in, w_dw, b_dw, cg, cb, w_out, b_out, lg, lb)


def _conv_sample_kernel(x_ref, st_ref, win_ref, bin_ref, wdw_ref, bdw_ref, cg_ref, cb_ref, wout_ref, bout_ref,
                        lg_ref, lb_ref, o_ref, ns_ref, s_ref, u_ref, c_ref, *, bsz, t_len):
    base = HALO_PAD - HALO
    x = x_ref[...]
    h = _dot(x.astype(BF16), win_ref[...]) + bin_ref[...]
    u_ref[...] = h[:, :D_MODEL] * _sigmoid(h[:, D_MODEL:])
    s_ref[:, 0:HALO_PAD, :] = jnp.zeros((bsz, HALO_PAD, D_MODEL), F32)
    s_ref[:, base:HALO_PAD, :] = st_ref[...]
    for b in range(bsz):
        s_ref[b, HALO_PAD:HALO_PAD + t_len, :] = u_ref[b * t_len:(b + 1) * t_len, :]
    acc = jnp.broadcast_to(bdw_ref[...][None], (bsz, t_len, D_MODEL))
    for k in range(CONV_WIDTH):
        acc = acc + wdw_ref[k:k + 1, :][None] * s_ref[:, base + k:base + k + t_len, :]
    for b in range(bsz):
        c_ref[b * t_len:(b + 1) * t_len, :] = acc[b]
    c = _silu(_ln(c_ref[...], cg_ref[...], cb_ref[...]))
    mix = _dot(c.astype(BF16), wout_ref[...]) + bout_ref[...]
    o_ref[...] = _ln(ALPHA * x + mix, lg_ref[...], lb_ref[...])
    ns_ref[...] = s_ref[:, base + t_len:HALO_PAD + t_len, :]


def _conv_layer_sample(x, state, w_in, b_in, w_dw, b_dw, cg, cb, w_out, b_out, lg, lb):
    bsz, t_len, _ = x.shape
    n = bsz * t_len
    assert t_len <= 8
    out, ns = pl.pallas_call(
        functools.partial(_conv_sample_kernel, bsz=bsz, t_len=t_len),
        out_shape=[jax.ShapeDtypeStruct((n, D_MODEL), F32), jax.ShapeDtypeStruct((bsz, HALO, D_MODEL), F32)],
        scratch_shapes=[pltpu.VMEM((bsz, HALO_PAD + 8, D_MODEL), F32), pltpu.VMEM((n, D_MODEL), F32),
                        pltpu.VMEM((n, D_MODEL), F32)],
        compiler_params=pltpu.CompilerParams(vmem_limit_bytes=VMEM_LIMIT),
        name="conv_layer_sample",
    )(x.reshape(n, D_MODEL), state, w_in, b_in, w_dw, b_dw, cg, cb, w_out, b_out, lg, lb)
    return out.reshape(x.shape), ns


def _ffn_kernel(x_ref, wg_ref, wu_ref, wd_ref, g_ref, b_ref, o_ref, acc_ref, xb_ref):
    f = pl.program_id(1)

    @pl.when(f == 0)
    def _():
        acc_ref[...] = jnp.zeros_like(acc_ref)
        xb_ref[...] = x_ref[...].astype(BF16)

    xb = xb_ref[...]
    h = _silu(_dot(xb, wg_ref[...])) * _dot(xb, wu_ref[...])
    acc_ref[...] += _dot(h.astype(BF16), wd_ref[...])

    @pl.when(f == pl.num_programs(1) - 1)
    def _():
        o_ref[...] = _ln(ALPHA * x_ref[...] + acc_ref[...], g_ref[...], b_ref[...])


def _ffn_layer(x, wg, wu, wd, g, b, *, tm=512, n_f=2):
    n = x.shape[0]
    d_ff = wg.shape[1]
    tm = min(tm, n)
    tf = d_ff // n_f
    assert n % tm == 0 and d_ff % n_f == 0 and tf % LANES == 0
    return pl.pallas_call(
        _ffn_kernel,
        grid=(n // tm, n_f),
        in_specs=[
            pl.BlockSpec((tm, D_MODEL), lambda i, f: (i, 0)),
            pl.BlockSpec((D_MODEL, tf), lambda i, f: (0, f)),
            pl.BlockSpec((D_MODEL, tf), lambda i, f: (0, f)),
            pl.BlockSpec((tf, D_MODEL), lambda i, f: (f, 0)),
            _full((1, D_MODEL)), _full((1, D_MODEL)),
        ],
        out_specs=pl.BlockSpec((tm, D_MODEL), lambda i, f: (i, 0)),
        out_shape=jax.ShapeDtypeStruct((n, D_MODEL), F32),
        scratch_shapes=[pltpu.VMEM((tm, D_MODEL), F32), pltpu.VMEM((tm, D_MODEL), BF16)],
        compiler_params=_params(("arbitrary", "arbitrary")),
        name="ffn_dense",
    )(x, wg, wu, wd, g, b)


def _rope_tables(pos):
    half = HEAD_DIM // 2
    inv = 1.0 / (ROPE_THETA ** (jnp.arange(half, dtype=F32) * (2.0 / HEAD_DIM)))
    ang = pos.astype(F32)[:, None] * inv[None, :]
    cos, sin = jnp.cos(ang), jnp.sin(ang)
    return jnp.tile(cos, (1, 4)), jnp.tile(jnp.concatenate([-sin, sin], axis=1), (1, 2))


def _rope(x, cos, sin, first_half):
    half = HEAD_DIM // 2
    out = []
    for c0 in range(0, x.shape[1], LANES):
        slab = x[:, c0:c0 + LANES]
        partner = jnp.where(first_half, pltpu.roll(slab, LANES - half, 1), pltpu.roll(slab, half, 1))
        out.append(slab * cos + partner * sin)
    return jnp.concatenate(out, axis=1)


def _store_dilated(dst_ref, val, s_ref, dil):
    if dil == 1:
        dst_ref[...] = val.astype(BF16)
        return
    d_att = N_HEADS * HEAD_DIM
    rows = val.shape[0] // dil
    for c in range(d_att // LANES):
        s_ref[c] = val[:, c * LANES:(c + 1) * LANES]
    for r in range(dil):
        for c in range(d_att // LANES):
            col = r * d_att + c * LANES
            dst_ref[:, col:col + LANES] = s_ref[c, pl.ds(r, rows, stride=dil), :].astype(BF16)


def _qkv_kernel(x_ref, cos_ref, sin_ref, wkv_ref, wq_ref, kv_ref, *rest, dilated):
    xb = x_ref[...].astype(BF16)
    cos, sin = cos_ref[...], sin_ref[...]
    first_half = (lax.broadcasted_iota(jnp.int32, cos.shape, 1) % HEAD_DIM) < HEAD_DIM // 2
    d_att = N_HEADS * HEAD_DIM
    kv = _dot(xb, wkv_ref[...])
    k = _rope(kv[:, :d_att], cos, sin, first_half)
    v = kv[:, d_att:]
    kv_ref[:, :d_att] = k
    kv_ref[:, d_att:] = v
    q = _rope(_dot(xb, wq_ref[...]), cos, sin, first_half) * (HEAD_DIM ** -0.5)
    if not dilated:
        rest[0][...] = q.astype(BF16)
        return
    s_ref = rest[-1]
    for g, (_, dil) in enumerate(GROUPS):
        q_ref, k_ref, v_ref = rest[3 * g:3 * g + 3]
        _store_dilated(q_ref, q[:, g * d_att:(g + 1) * d_att], s_ref, dil)
        _store_dilated(k_ref, k, s_ref, dil)
        _store_dilated(v_ref, v, s_ref, dil)


def _qkv_layer(x, pos, w_kv, w_q, *, dilated, tm=256):
    n = x.shape[0]
    tm = min(tm, n)
    assert n % tm == 0
    d_att = N_HEADS * HEAD_DIM
    cos, sin = _rope_tables(pos)
    out_specs = [pl.BlockSpec((tm, 2 * d_att), lambda i: (i, 0))]
    out_shape = [jax.ShapeDtypeStruct((n, 2 * d_att), F32)]
    if dilated:
        for _, dil in GROUPS:
            assert tm % (16 * dil) == 0
            out_specs += [pl.BlockSpec((tm // dil, dil * d_att), lambda i: (i, 0))] * 3
            out_shape += [jax.ShapeDtypeStruct((n // dil, dil * d_att), BF16)] * 3
    else:
        out_specs.append(pl.BlockSpec((tm, N_GROUPS * d_att), lambda i: (i, 0)))
        out_shape.append(jax.ShapeDtypeStruct((n, N_GROUPS * d_att), BF16))
    return pl.pallas_call(
        functools.partial(_qkv_kernel, dilated=dilated),
        grid=(n // tm,),
        in_specs=[
            pl.BlockSpec((tm, D_MODEL), lambda i: (i, 0)),
            pl.BlockSpec((tm, LANES), lambda i: (i, 0)),
            pl.BlockSpec((tm, LANES), lambda i: (i, 0)),
            _full((D_MODEL, 2 * d_att)), _full((D_MODEL, N_GROUPS * d_att)),
        ],
        out_specs=out_specs,
        out_shape=out_shape,
        scratch_shapes=[pltpu.VMEM((d_att // LANES, tm, LANES), F32)] if dilated else [],
        compiler_params=_params(("arbitrary",)),
        name="qkv_rope",
    )(x, cos, sin, w_kv, w_q)


def _attn_prompt_kernel(q_ref, kp_ref, kc_ref, vp_ref, vc_ref, o_ref, lse_ref, *, qb):
    blk = pl.program_id(1)
    a = lax.broadcasted_iota(jnp.int32, (qb, 2 * qb), 0)
    c = lax.broadcasted_iota(jnp.int32, (qb, 2 * qb), 1)
    valid = (c >= a) & (c <= a + KEYS_PER_GROUP) & ((blk > 0) | (c >= qb))
    lane = lax.broadcasted_iota(jnp.int32, (qb, LANES), 1)
    low = lane < HEAD_DIM
    lse_all = jnp.zeros((qb, LANES), F32)
    for p in range(N_HEADS // 2):
        sl = slice(p * LANES, (p + 1) * LANES)
        qp = q_ref[:, sl]
        kk = jnp.concatenate([kp_ref[:, sl], kc_ref[:, sl]], axis=0)
        vv = jnp.concatenate([vp_ref[:, sl], vc_ref[:, sl]], axis=0)
        outs = []
        for j in range(2):
            qm = jnp.where(low if j == 0 else jnp.logical_not(low), qp, jnp.zeros_like(qp))
            s = lax.dot_general(qm, kk, (((1,), (1,)), ((), ())), preferred_element_type=F32)
            s = jnp.where(valid, s, NEG)
            m = jnp.max(s, axis=1, keepdims=True)
            e = jnp.exp(s - m)
            den = jnp.sum(e, axis=1, keepdims=True)
            outs.append(_dot(e.astype(BF16), vv) / den)
            lse_all = jnp.where(lane == 2 * p + j, m + jnp.log(den), lse_all)
        o_ref[:, sl] = jnp.where(low, outs[0], outs[1]).astype(BF16)
    lse_ref[0] = lse_all


def _attn_prompt_group(qg, kg, vg, g, *, qb=128):
    dil = GROUPS[g][1]
    assert GROUPS[g][0] // dil == KEYS_PER_GROUP and qb == KEYS_PER_GROUP
    d_att = N_HEADS * HEAD_DIM
    rows = kg.shape[0]
    assert kg.shape[1] == dil * d_att and rows % qb == 0
    cur = pl.BlockSpec((qb, d_att), lambda r, b: (b, r))
    prev = pl.BlockSpec((qb, d_att), lambda r, b: (jnp.maximum(b - 1, 0), r))
    return pl.pallas_call(
        functools.partial(_attn_prompt_kernel, qb=qb),
        grid=(dil, rows // qb),
        in_specs=[cur, prev, cur, prev, cur],
        out_specs=[cur, pl.BlockSpec((1, qb, LANES), lambda r, b: (r, b, 0))],
        out_shape=[jax.ShapeDtypeStruct((rows, dil * d_att), BF16), jax.ShapeDtypeStruct((dil, rows, LANES), F32)],
        compiler_params=_params(("arbitrary", "arbitrary")),
        name=f"attn_prompt_g{g}",
    )(qg, kg, kg, vg, vg)


SAMPLE_ROWS = 16


def _sample_valid(t_len, past):
    key = np.arange(past + LANES)
    valid = np.zeros((SAMPLE_ROWS, past + LANES), np.float32)
    for r in range(N_GROUPS * t_len):
        g, i = divmod(r, t_len)
        window, dil = GROUPS[g]
        dist = past + i - key
        ok = (dist >= 0) & (dist <= window) & (dist % dil == 0) & (key < past + t_len)
        assert ok.sum() == window // dil + 1
        valid[r] = ok
    return valid[:, :past], valid[:, past:]


def _group_reduce(x, op):
    quarter = SAMPLE_ROWS // 4
    return op(op(x, pltpu.roll(x, quarter, 0)), op(pltpu.roll(x, 2 * quarter, 0), pltpu.roll(x, 3 * quarter, 0)))


def _attn_sample_kernel(q_ref, kvn_ref, vc_ref, vn_ref, cache_ref, o_ref, *, t_len):
    d_att = N_HEADS * HEAD_DIM
    q = q_ref[0]
    q_rows = jnp.concatenate([q[:, g * d_att:(g + 1) * d_att] for g in range(N_GROUPS)]
                             + [jnp.zeros((SAMPLE_ROWS - N_GROUPS * t_len, d_att), BF16)], axis=0)
    kvn = jnp.concatenate([kvn_ref[0], jnp.zeros((LANES - t_len, 2 * d_att), F32)], axis=0).astype(BF16)
    valid_c, valid_n = vc_ref[...] > 0.0, vn_ref[...] > 0.0
    ones = jnp.ones((SAMPLE_ROWS, LANES), F32)
    outs = []
    for h in range(N_HEADS):
        sl = slice(h * HEAD_DIM, (h + 1) * HEAD_DIM)
        qh = q_rows[:, sl]
        kt = cache_ref[0, 0, h].astype(BF16)
        vt = cache_ref[0, 1, h].astype(BF16)
        s_c = jnp.where(valid_c, _dot(qh, kt), NEG)
        s_n = jnp.where(valid_n, lax.dot_general(qh, kvn[:, sl], (((1,), (1,)), ((), ())),
                                                 preferred_element_type=F32), NEG)
        m = jnp.maximum(jnp.max(s_c, axis=1, keepdims=True), jnp.max(s_n, axis=1, keepdims=True))
        m = _group_reduce(m * ones, jnp.maximum)[:, 0:1]
        e_c, e_n = jnp.exp(s_c - m), jnp.exp(s_n - m)
        den = jnp.sum(e_c, axis=1, keepdims=True) + jnp.sum(e_n, axis=1, keepdims=True)
        den = _group_reduce(den * ones, jnp.add)[:, 0:1]
        o = lax.dot_general((e_c / den).astype(BF16), vt, (((1,), (1,)), ((), ())), preferred_element_type=F32)
        o = o + _dot((e_n / den).astype(BF16), kvn[:, d_att + h * HEAD_DIM:d_att + (h + 1) * HEAD_DIM])
        out = o[0:t_len]
        for g in range(1, N_GROUPS):
            out = out + o[g * t_len:(g + 1) * t_len]
        outs.append(out)
    o_ref[0] = jnp.concatenate(outs, axis=1)


def _attn_sample(q, kv_new, cache_t):
    bsz, t_len, _ = q.shape
    past = cache_t.shape[-1]
    d_att = N_HEADS * HEAD_DIM
    assert N_GROUPS * t_len <= SAMPLE_ROWS and SAMPLE_ROWS % 4 == 0 and SAMPLE_ROWS // 4 == t_len
    assert PAST_LEN >= past and past >= max(w for w, _ in GROUPS)
    valid_c, valid_n = _sample_valid(t_len, past)
    return pl.pallas_call(
        functools.partial(_attn_sample_kernel, t_len=t_len),
        grid=(bsz,),
        in_specs=[
            pl.BlockSpec((1, t_len, N_GROUPS * d_att), lambda b: (b, 0, 0)),
            pl.BlockSpec((1, t_len, 2 * d_att), lambda b: (b, 0, 0)),
            _full(valid_c.shape), _full(valid_n.shape),
            pl.BlockSpec((1, 2, N_HEADS, HEAD_DIM, past), lambda b: (b, 0, 0, 0, 0)),
        ],
        out_specs=pl.BlockSpec((1, t_len, d_att), lambda b: (b, 0, 0)),
        out_shape=jax.ShapeDtypeStruct((bsz, t_len, d_att), F32),
        compiler_params=_params(("arbitrary",)),
        name="attn_sample",
    )(q, kv_new, jnp.asarray(valid_c), jnp.asarray(valid_n), cache_t)


def _store_token_tiles(ref, x):
    n = x.shape[0]
    for s in range(TILE_ROWS):
        ref[pl.ds(s, n, stride=TILE_ROWS), :] = x[:, s * LANES:(s + 1) * LANES]


def _load_token_slab(ref, n, s):
    return ref[pl.ds(s, n, stride=TILE_ROWS), :]


def _token_rows(ref, t):
    return ref.at[pl.ds(pl.multiple_of(t * TILE_ROWS, TILE_ROWS), TILE_ROWS), :]


def _split3(w):
    hi = w.astype(BF16)
    r1 = w - hi.astype(F32)
    mid = r1.astype(BF16)
    lo = (r1 - mid.astype(F32)).astype(BF16)
    return hi, mid, lo


def _oproj_router_tail(att, x_ref, wo_ref, g_ref, b_ref, wrh_ref, wrl_ref, x3t_ref, route_ref, cnt_ref):
    tm = att.shape[0]

    @pl.when(pl.program_id(0) == 0)
    def _():
        cnt_ref[...] = jnp.zeros_like(cnt_ref)

    x3 = _ln(ALPHA * x_ref[...] + _dot(att.astype(BF16), wo_ref[...]), g_ref[...], b_ref[...])
    _store_token_tiles(x3t_ref, x3)
    xh = x3.astype(BF16)
    xl = (x3 - xh.astype(F32)).astype(BF16)
    logits = _dot(xh, wrh_ref[...]) + (_dot(xh, wrl_ref[...]) + _dot(xl, wrh_ref[...]))
    lane = lax.broadcasted_iota(jnp.int32, logits.shape, 1)
    logits = jnp.where(lane < N_EXPERTS, logits, NEG)
    v1 = jnp.max(logits, axis=1, keepdims=True)
    i1 = jnp.min(jnp.where(logits == v1, lane, LANES), axis=1, keepdims=True)
    rest = jnp.where(lane == i1, NEG, logits)
    v2 = jnp.max(rest, axis=1, keepdims=True)
    i2 = jnp.min(jnp.where(rest == v2, lane, LANES), axis=1, keepdims=True)
    e2 = jnp.exp(v2 - v1)
    den = 1.0 + e2
    sel = jnp.where((lane == i1) | (lane == i2), 1.0, 0.0)
    tri = jnp.where(lax.broadcasted_iota(jnp.int32, (tm, tm), 1) < lax.broadcasted_iota(jnp.int32, (tm, tm), 0),
                    1.0, 0.0).astype(BF16)
    before = _dot(tri, sel.astype(BF16)) + cnt_ref[0:1, :]
    rank1 = jnp.sum(jnp.where(lane == i1, before, 0.0), axis=1, keepdims=True)
    rank2 = jnp.sum(jnp.where(lane == i2, before, 0.0), axis=1, keepdims=True)
    cnt_ref[0:1, :] = cnt_ref[0:1, :] + jnp.sum(sel, axis=0, keepdims=True)
    fields = (i1.astype(F32), i2.astype(F32), 1.0 / den, e2 / den, rank1, rank2)
    route = jnp.zeros(logits.shape, F32)
    for k, val in enumerate(fields):
        route = jnp.where(lane == k, val, route)
    route_ref[...] = route


def _load_dilated(o_ref, lse_ref, o_scr, l_scr, dil):
    if dil == 1:
        return o_ref[...].astype(F32), lse_ref[0]
    d_att = N_HEADS * HEAD_DIM
    rows = o_ref.shape[0]
    n_slabs = d_att // LANES
    for r in range(dil):
        for c in range(n_slabs):
            col = r * d_att + c * LANES
            o_scr[c, pl.ds(r, rows, stride=dil), :] = o_ref[:, col:col + LANES].astype(F32)
        l_scr[pl.ds(r, rows, stride=dil), :] = lse_ref[r]
    return jnp.concatenate([o_scr[c] for c in range(n_slabs)], axis=1), l_scr[...]


def _oproj_prompt_kernel(o0_ref, l0_ref, o1_ref, l1_ref, o2_ref, l2_ref, ex_ref, x_ref, wo_ref, g_ref, b_ref,
                         wrh_ref, wrl_ref, x3t_ref, route_ref, cnt_ref, *scr):
    parts = [_load_dilated(o_ref, l_ref, scr[2 * g], scr[2 * g + 1], GROUPS[g][1])
             for g, (o_ref, l_ref) in enumerate(((o0_ref, l0_ref), (o1_ref, l1_ref), (o2_ref, l2_ref)))]
    lse = [l for _, l in parts]
    m = jnp.maximum(jnp.maximum(lse[0], lse[1]), lse[2])
    w = [jnp.exp(l - m) for l in lse]
    den = w[0] + w[1] + w[2]
    att = None
    for g, (o, _) in enumerate(parts):
        hi, mid, lo = _split3(w[g] / den)
        wexp = _dot(hi, ex_ref[...]) + (_dot(mid, ex_ref[...]) + _dot(lo, ex_ref[...]))
        term = wexp * o
        att = term if att is None else att + term
    _oproj_router_tail(att, x_ref, wo_ref, g_ref, b_ref, wrh_ref, wrl_ref, x3t_ref, route_ref, cnt_ref)


def _oproj_sample_kernel(att_ref, x_ref, wo_ref, g_ref, b_ref, wrh_ref, wrl_ref, x3t_ref, route_ref, cnt_ref):
    _oproj_router_tail(att_ref[...], x_ref, wo_ref, g_ref, b_ref, wrh_ref, wrl_ref, x3t_ref, route_ref, cnt_ref)


def _router_split(w_router):
    wr = jnp.zeros((D_MODEL, LANES), F32).at[:, :N_EXPERTS].set(w_router)
    hi = wr.astype(BF16)
    return hi, (wr - hi.astype(F32)).astype(BF16)


def _oproj_router(att_parts, x, w_o, g, b, w_router, *, tm=256):
    n = x.shape[0]
    tm = min(tm, n)
    assert n % tm == 0
    d_att = N_HEADS * HEAD_DIM
    wrh, wrl = _router_split(w_router)
    row_blk = lambda w: pl.BlockSpec((tm, w), lambda i: (i, 0))
    tail_specs = [row_blk(D_MODEL), _full((d_att, D_MODEL)), _full((1, D_MODEL)), _full((1, D_MODEL)),
                  _full((D_MODEL, LANES)), _full((D_MODEL, LANES))]
    tail_args = (x, w_o, g, b, wrh, wrl)
    scratch = []
    if len(att_parts) == 1:
        kern, specs, args = _oproj_sample_kernel, [row_blk(d_att)], tuple(att_parts)
    else:
        expand = (np.arange(LANES)[:, None] == np.arange(d_att)[None, :] // HEAD_DIM).astype(np.float32)
        kern, specs, args = _oproj_prompt_kernel, [], []
        for (_, dil), (o, lse) in zip(GROUPS, att_parts):
            specs += [pl.BlockSpec((tm // dil, dil * d_att), lambda i: (i, 0)),
                      pl.BlockSpec((dil, tm // dil, LANES), lambda i: (0, i, 0))]
            args += [o, lse]
            scratch += [pltpu.VMEM((d_att // LANES, tm, LANES), F32), pltpu.VMEM((tm, LANES), F32)]
        specs.append(_full((LANES, d_att)))
        args.append(jnp.asarray(expand, BF16))
    return pl.pallas_call(
        kern,
        grid=(n // tm,),
        in_specs=specs + tail_specs,
        out_specs=[pl.BlockSpec((tm * TILE_ROWS, LANES), lambda i: (i, 0)), row_blk(LANES), _full((TILE_ROWS, LANES))],
        out_shape=[jax.ShapeDtypeStruct((n * TILE_ROWS, LANES), F32), jax.ShapeDtypeStruct((n, LANES), F32),
                   jax.ShapeDtypeStruct((TILE_ROWS, LANES), F32)],
        scratch_shapes=scratch,
        compiler_params=_params(("arbitrary",)),
        name="oproj_router",
    )(*args, *tail_args)


def _route_plan(routes, counts, tile, n_tiles):
    cnt = [c[0, :N_EXPERTS].astype(jnp.int32) for c in counts]
    total = sum(cnt)
    padded = (total + tile - 1) // tile * tile
    ends = jnp.cumsum(padded)
    offs = ends - padded
    fill = jnp.concatenate([offs + total, ends]).astype(jnp.int32)
    pos = []
    for route, c in zip(routes, cnt):
        e = route[:, 0:2].astype(jnp.int32)
        pos.append((offs[e] + route[:, 4:6].astype(jnp.int32)).reshape(-1))
        offs = offs + c
    n_live = (ends[-1] // tile).astype(jnp.int32)
    tile_id = jnp.minimum(jnp.arange(n_tiles, dtype=jnp.int32), n_live - 1)
    tile_expert = jnp.sum((ends // tile)[None, :] <= tile_id[:, None], axis=1).astype(jnp.int32)
    return pos, tile_expert, n_live.reshape(1), fill


def _dispatch_kernel(pos_ref, fill_ref, xa_ref, xb_ref, xs_ref, z_ref, sem, *, tm, steps_a, n_b, tile, n_tiles):
    i = pl.program_id(0)

    def scatter(src_ref, count, first_token):
        def row_copy(r, s):
            dst = _token_rows(xs_ref, pos_ref[(first_token + r) * 2 + s])
            return pltpu.make_async_copy(_token_rows(src_ref, r), dst, sem)

        def issue(r, c):
            row_copy(r, 0).start(priority=0)
            row_copy(r, 1).start(priority=1)
            return c

        def drain(r, c):
            row_copy(r, 0).wait()
            row_copy(r, 1).wait()
            return c

        lax.fori_loop(0, count, issue, 0)
        lax.fori_loop(0, count, drain, 0)

    @pl.when(i < steps_a)
    def _():
        scatter(xa_ref, tm, i * tm)

    @pl.when(i == steps_a)
    def _():
        scatter(xb_ref, n_b, steps_a * tm)
        z_ref[...] = jnp.zeros_like(z_ref)
        zero_row = lambda r: pltpu.make_async_copy(z_ref.at[0:TILE_ROWS, :], _token_rows(xs_ref, r), sem)
        tile_rows = tile * TILE_ROWS
        zero_tile = lambda j: pltpu.make_async_copy(
            z_ref, xs_ref.at[pl.ds(pl.multiple_of(j * tile_rows, tile_rows), tile_rows), :], sem)

        def loop(lo, hi, copy):
            def issue(r, c):
                copy(r).start()
                return c

            def drain(r, c):
                copy(r).wait()
                return c

            lax.fori_loop(lo, hi, issue, 0)
            lax.fori_loop(lo, hi, drain, 0)

        for e in range(N_EXPERTS):
            loop(fill_ref[e], fill_ref[N_EXPERTS + e], zero_row)
        loop(fill_ref[2 * N_EXPERTS - 1] // tile, n_tiles, zero_tile)


def _dispatch(pos, fill_rng, x3t_a, x3t_b, n_tiles, tile, *, tm=256):
    n_a, n_b = x3t_a.shape[0] // TILE_ROWS, x3t_b.shape[0] // TILE_ROWS
    tm = min(tm, n_a)
    assert n_a % tm == 0
    steps_a = n_a // tm
    return pl.pallas_call(
        functools.partial(_dispatch_kernel, tm=tm, steps_a=steps_a, n_b=n_b, tile=tile, n_tiles=n_tiles),
        grid_spec=pltpu.PrefetchScalarGridSpec(
            num_scalar_prefetch=2, grid=(steps_a + 1,),
            in_specs=[pl.BlockSpec((tm * TILE_ROWS, LANES), lambda i, *_: (jnp.minimum(i, steps_a - 1), 0)),
                      pl.BlockSpec((n_b * TILE_ROWS, LANES), lambda i, *_: (0, 0))],
            out_specs=pl.BlockSpec(memory_space=pl.ANY),
            scratch_shapes=[pltpu.VMEM((tile * TILE_ROWS, LANES), F32), pltpu.SemaphoreType.DMA(())]),
        out_shape=jax.ShapeDtypeStruct((n_tiles * tile * TILE_ROWS, LANES), F32),
        compiler_params=_params(("arbitrary",)),
        name="moe_dispatch",
    )(pos, fill_rng, x3t_a, x3t_b)


def _moe_ffn_kernel(te_ref, nl_ref, xs_ref, wg_ref, wu_ref, wd_ref, ys_ref, acc_ref, xb_ref, *, tile):
    j, f = pl.program_id(0), pl.program_id(1)

    @pl.when(j < nl_ref[0])
    def _():
        @pl.when(f == 0)
        def _():
            for s in range(TILE_ROWS):
                xb_ref[:, s * LANES:(s + 1) * LANES] = _load_token_slab(xs_ref, tile, s).astype(BF16)

        xb = xb_ref[...]
        h = _silu(_dot(xb, wg_ref[...])) * _dot(xb, wu_ref[...])
        y = _dot(h.astype(BF16), wd_ref[...])

        @pl.when(f == 0)
        def _():
            acc_ref[...] = y

        @pl.when(f > 0)
        def _():
            acc_ref[...] += y

        @pl.when(f == pl.num_programs(1) - 1)
        def _():
            _store_token_tiles(ys_ref, acc_ref[...])

    @pl.when((j >= nl_ref[0]) & (f == pl.num_programs(1) - 1))
    def _():
        ys_ref[...] = jnp.zeros_like(ys_ref)


def _moe_ffn(xs, tile_expert, n_live, wg, wu, wd, *, tile, n_f=2):
    n_tiles = tile_expert.shape[0]
    d_ff = wg.shape[2]
    tf = d_ff // n_f
    assert d_ff % n_f == 0 and tf % LANES == 0 and xs.shape[0] == n_tiles * tile * TILE_ROWS
    row_map = lambda j, f, te, nl: (jnp.minimum(j, nl[0] - 1), 0)
    f_idx = lambda j, f, nl: jnp.where(j < nl[0], f, n_f - 1)
    return pl.pallas_call(
        functools.partial(_moe_ffn_kernel, tile=tile),
        grid_spec=pltpu.PrefetchScalarGridSpec(
            num_scalar_prefetch=2, grid=(n_tiles, n_f),
            in_specs=[
                pl.BlockSpec((tile * TILE_ROWS, LANES), row_map),
                pl.BlockSpec((None, D_MODEL, tf), lambda j, f, te, nl: (te[j], 0, f_idx(j, f, nl))),
                pl.BlockSpec((None, D_MODEL, tf), lambda j, f, te, nl: (te[j], 0, f_idx(j, f, nl))),
                pl.BlockSpec((None, tf, D_MODEL), lambda j, f, te, nl: (te[j], f_idx(j, f, nl), 0)),
            ],
            out_specs=pl.BlockSpec((tile * TILE_ROWS, LANES), lambda j, f, te, nl: (j, 0)),
            scratch_shapes=[pltpu.VMEM((tile, D_MODEL), F32), pltpu.VMEM((tile, D_MODEL), BF16)]),
        out_shape=jax.ShapeDtypeStruct(xs.shape, F32),
        compiler_params=_params(("arbitrary", "arbitrary")),
        name="moe_ffn",
    )(tile_expert, n_live, xs, wg, wu, wd)


def _moe_combine_kernel(pos_ref, route_ref, x3t_ref, ys_ref, g_ref, b_ref, o_ref, y1_ref, y2_ref, pre_ref, sem, *, tm):
    i = pl.program_id(0)
    bufs = (y1_ref, y2_ref)

    def row_copy(r, s):
        return pltpu.make_async_copy(_token_rows(ys_ref, pos_ref[(i * tm + r) * 2 + s]), _token_rows(bufs[s], r), sem)

    def issue(r, c):
        row_copy(r, 0).start(priority=0)
        row_copy(r, 1).start(priority=1)
        return c

    def drain(r, c):
        row_copy(r, 0).wait()
        row_copy(r, 1).wait()
        return c

    lax.fori_loop(0, tm, issue, 0)
    lax.fori_loop(0, tm, drain, 0)
    route = route_ref[...]
    g1, g2 = route[:, 2:3], route[:, 3:4]
    for s in range(TILE_ROWS):
        y = g1 * _load_token_slab(y1_ref, tm, s) + g2 * _load_token_slab(y2_ref, tm, s)
        pre_ref[:, s * LANES:(s + 1) * LANES] = ALPHA * _load_token_slab(x3t_ref, tm, s) + y
    o_ref[...] = _ln(pre_ref[...], g_ref[...], b_ref[...])


def _moe_combine(pos, route, x3t, ys, g, b, *, tm=256):
    n = route.shape[0]
    tm = min(tm, n)
    assert n % tm == 0
    tok_buf = pltpu.VMEM((tm * TILE_ROWS, LANES), F32)
    return pl.pallas_call(
        functools.partial(_moe_combine_kernel, tm=tm),
        grid_spec=pltpu.PrefetchScalarGridSpec(
            num_scalar_prefetch=1, grid=(n // tm,),
            in_specs=[
                pl.BlockSpec((tm, LANES), lambda i, *_: (i, 0)),
                pl.BlockSpec((tm * TILE_ROWS, LANES), lambda i, *_: (i, 0)),
                pl.BlockSpec(memory_space=pl.ANY),
                pl.BlockSpec((1, D_MODEL), lambda i, *_: (0, 0)),
                pl.BlockSpec((1, D_MODEL), lambda i, *_: (0, 0)),
            ],
            out_specs=pl.BlockSpec((tm, D_MODEL), lambda i, *_: (i, 0)),
            scratch_shapes=[tok_buf, tok_buf, pltpu.VMEM((tm, D_MODEL), F32), pltpu.SemaphoreType.DMA(())]),
        out_shape=jax.ShapeDtypeStruct((n, D_MODEL), F32),
        compiler_params=_params(("arbitrary",)),
        name="moe_combine",
    )(pos, route, x3t, ys, g, b)


def kernel(x_prompt, x_sample, cache_kv, state_conv, conv_w_in, conv_b_in, conv_w_dw, conv_b_dw, conv_ln_g,
           conv_ln_b, conv_w_out, conv_b_out, ffn_w_gate, ffn_w_up, ffn_w_down, w_kv, attn_w_q, attn_w_o,
           moe_w_router, moe_w_gate, moe_w_up, moe_w_down, ln_g, ln_b):
    bp, t_p, _ = x_prompt.shape
    bs, t_s, _ = x_sample.shape
    assert bp == 1 and conv_w_in.shape[0] == 1 and attn_w_q.shape[0] == 1
    d_att = N_HEADS * HEAD_DIM
    row = lambda a: a.reshape(1, -1)
    bf = lambda a: a.astype(BF16)

    conv_w = (bf(conv_w_in[0]), row(conv_b_in[0]), conv_w_dw[0], row(conv_b_dw[0]), row(conv_ln_g[0]),
              row(conv_ln_b[0]), bf(conv_w_out[0]), row(conv_b_out[0]), row(ln_g[0, 0]), row(ln_b[0, 0]))
    ffn_w = (bf(ffn_w_gate[0]), bf(ffn_w_up[0]), bf(ffn_w_down[0]), row(ln_g[0, 1]), row(ln_b[0, 1]))
    wkv_b, wq_b, wo_b = bf(w_kv), bf(attn_w_q[0]), bf(attn_w_o[0])
    ln10 = (row(ln_g[1, 0]), row(ln_b[1, 0]))
    ln11 = (row(ln_g[1, 1]), row(ln_b[1, 1]))

    x1, conv_p = _conv_layer_prompt(x_prompt, jnp.zeros((bp, HALO, D_MODEL), F32), *conv_w)
    x2 = _ffn_layer(x1.reshape(t_p, D_MODEL), *ffn_w)
    kv_p, *qkv_views = _qkv_layer(x2, jnp.arange(t_p, dtype=jnp.int32), wkv_b, wq_b, dilated=True)
    parts = [_attn_prompt_group(*qkv_views[3 * g:3 * g + 3], g) for g in range(N_GROUPS)]
    x3t_p, route_p, cnt_p = _oproj_router(parts, x2, wo_b, *ln10, moe_w_router[0])
    n_keep = min(max(w for w, _ in GROUPS), t_p)
    kv_prompt = kv_p[t_p - n_keep:].reshape(bp, n_keep, 2, N_HEADS, HEAD_DIM)

    n_s = bs * t_s
    x1s, conv_s = _conv_layer_sample(x_sample, state_conv[0], *conv_w)
    x2s = _ffn_layer(x1s.reshape(n_s, D_MODEL), *ffn_w)
    pos_s = PAST_LEN + jnp.tile(jnp.arange(t_s, dtype=jnp.int32), bs)
    kv_s, q_s = _qkv_layer(x2s, pos_s, wkv_b, wq_b, dilated=False)
    att_s = _attn_sample(q_s.reshape(bs, t_s, N_GROUPS * d_att), kv_s.reshape(bs, t_s, 2 * d_att),
                         jnp.transpose(cache_kv, (0, 2, 3, 4, 1)))
    x3t_s, route_s, cnt_s = _oproj_router((att_s.reshape(n_s, d_att),), x2s, wo_b, *ln10, moe_w_router[0])
    kv_sample = kv_s.reshape(bs, t_s, 2, N_HEADS, HEAD_DIM)

    n_slots = 2 * (t_p + n_s)
    n_tiles = (n_slots + N_EXPERTS * (MOE_TILE - 1)) // MOE_TILE
    (pos_p, pos_sm), tile_expert, n_live, fill_rng = _route_plan((route_p, route_s), (cnt_p, cnt_s), MOE_TILE, n_tiles)
    xs = _dispatch(jnp.concatenate([pos_p, pos_sm]), fill_rng, x3t_p, x3t_s, n_tiles, MOE_TILE)
    ys = _moe_ffn(xs, tile_expert, n_live, bf(moe_w_gate[0]), bf(moe_w_up[0]), bf(moe_w_down[0]), tile=MOE_TILE)
    y_prompt = _moe_combine(pos_p, route_p, x3t_p, ys, *ln11).reshape(bp, t_p, D_MODEL)
    y_sample = _moe_combine(pos_sm, route_s, x3t_s, ys, *ln11).reshape(bs, t_s, D_MODEL)

    return (y_prompt, y_sample, conv_p[None], conv_s[None], kv_prompt, kv_sample)
```

```python
import functools

import numpy as np
import jax
import jax.numpy as jnp
from jax import lax
from jax.experimental import pallas as pl
from jax.experimental.pallas import tpu as pltpu

F32 = jnp.float32
BF16 = jnp.bfloat16

D_MODEL = 1024
N_HEADS = 16
HEAD_DIM = 64
GROUPS = ((128, 1), (512, 4), (2048, 16))
N_GROUPS = len(GROUPS)
KEYS_PER_GROUP = 128
CONV_WIDTH = 31
HALO = CONV_WIDTH - 1
N_EXPERTS = 8
ROPE_THETA = 10000.0
LN_EPS = 1e-5
DEPTH = 2
ALPHA = (2 * DEPTH) ** 0.25
PAST_LEN = 16384
NEG = -1e30

LANES = 128
HALO_PAD = 32
TILE_ROWS = 8
MOE_TILE = 512
VMEM_LIMIT = 56 * 1024 * 1024


def _params(sem, vmem=VMEM_LIMIT):
    return pltpu.CompilerParams(dimension_semantics=sem, vmem_limit_bytes=vmem)


def _dot(a, b):
    return jnp.dot(a, b, preferred_element_type=F32)


def _sigmoid(x):
    return 1.0 / (1.0 + jnp.exp(-x))


def _silu(x):
    return x * _sigmoid(x)


def _ln(x, g, b):
    xc = x - jnp.mean(x, axis=-1, keepdims=True)
    var = jnp.mean(xc * xc, axis=-1, keepdims=True)
    return xc * lax.rsqrt(var + LN_EPS) * g + b


def _full(shape):
    return pl.BlockSpec(shape, lambda *_: (0,) * len(shape))


def _conv_taps(s_ref, wdw_ref, bdw_ref, c_ref, sh_ref, n_rows, row_chunk):
    base = HALO_PAD - HALO
    phases = [[k for k in range(CONV_WIDTH) if (base + k) % TILE_ROWS == ph] for ph in range(TILE_ROWS)]

    def chunk(rc, carry):
        r0 = pl.multiple_of(rc * row_chunk, row_chunk)
        win = s_ref.at[pl.ds(r0, row_chunk + HALO_PAD), :]
        for c0 in range(0, D_MODEL, LANES):
            for ph, taps in enumerate(phases):
                n_ext = row_chunk + taps[-1] - taps[0]
                sh_ref[ph, 0:n_ext, :] = win[base + taps[0]:base + taps[0] + n_ext, c0:c0 + LANES]
            acc = jnp.broadcast_to(bdw_ref[:, c0:c0 + LANES], (row_chunk, LANES))
            for ph, taps in enumerate(phases):
                for k in taps:
                    acc = acc + wdw_ref[k:k + 1, c0:c0 + LANES] * sh_ref[ph, k - taps[0]:k - taps[0] + row_chunk, :]
            c_ref[pl.ds(r0, row_chunk), c0:c0 + LANES] = acc
        return carry

    lax.fori_loop(0, n_rows // row_chunk, chunk, 0)


def _conv_prompt_kernel(x_ref, st_ref, win_ref, bin_ref, wdw_ref, bdw_ref, cg_ref, cb_ref, wout_ref, bout_ref,
                        lg_ref, lb_ref, o_ref, ns_ref, s_ref, c_ref, sh_ref, *, tm, row_chunk):
    base = HALO_PAD - HALO

    @pl.when(pl.program_id(1) == 0)
    def _():
        s_ref[0:HALO_PAD, :] = jnp.zeros((HALO_PAD, D_MODEL), F32)
        s_ref[base:HALO_PAD, :] = st_ref[0]

    x = x_ref[0]
    h = _dot(x.astype(BF16), win_ref[...]) + bin_ref[...]
    s_ref[HALO_PAD:HALO_PAD + tm, :] = h[:, :D_MODEL] * _sigmoid(h[:, D_MODEL:])
    _conv_taps(s_ref, wdw_ref, bdw_ref, c_ref, sh_ref, tm, row_chunk)
    c = _silu(_ln(c_ref[...], cg_ref[...], cb_ref[...]))
    mix = _dot(c.astype(BF16), wout_ref[...]) + bout_ref[...]
    o_ref[0] = _ln(ALPHA * x + mix, lg_ref[...], lb_ref[...])
    tail = s_ref[pl.ds(tm + base, HALO), :]
    s_ref[base:HALO_PAD, :] = tail
    ns_ref[0] = tail


def _conv_layer_prompt(x, state, w_in, b_in, w_dw, b_dw, cg, cb, w_out, b_out, lg, lb, *, tm=256):
    bsz, t_len, _ = x.shape
    tm = min(tm, t_len)
    assert t_len % tm == 0 and tm >= HALO_PAD and tm % 8 == 0
    row_chunk = 64 if tm % 64 == 0 else 8
    row = lambda n: _full((1, n))
    return pl.pallas_call(
        functools.partial(_conv_prompt_kernel, tm=tm, row_chunk=row_chunk),
        grid=(bsz, t_len // tm),
        in_specs=[
            pl.BlockSpec((1, tm, D_MODEL), lambda b, i: (b, i, 0)),
            pl.BlockSpec((1, HALO, D_MODEL), lambda b, i: (b, 0, 0)),
            _full((D_MODEL, 2 * D_MODEL)), row(2 * D_MODEL),
            _full((CONV_WIDTH, D_MODEL)), row(D_MODEL), row(D_MODEL), row(D_MODEL),
            _full((D_MODEL, D_MODEL)), row(D_MODEL), row(D_MODEL), row(D_MODEL),
        ],
        out_specs=[
            pl.BlockSpec((1, tm, D_MODEL), lambda b, i: (b, i, 0)),
            pl.BlockSpec((1, HALO, D_MODEL), lambda b, i: (b, 0, 0)),
        ],
        out_shape=[jax.ShapeDtypeStruct(x.shape, F32), jax.ShapeDtypeStruct((bsz, HALO, D_MODEL), F32)],
        scratch_shapes=[pltpu.VMEM((HALO_PAD + tm, D_MODEL), F32), pltpu.VMEM((tm, D_MODEL), F32),
                        pltpu.VMEM((TILE_ROWS, row_chunk + HALO_PAD, LANES), F32)],
        compiler_params=_params(("arbitrary", "arbitrary")),
        name="conv_layer_prompt",
    )(x, state, w_in, b_in, w_dw, b_dw, cg, cb, w_out, b_out, lg, lb)


def _conv_sample_kernel(x_ref, st_ref, win_ref, bin_ref, wdw_ref, bdw_ref, cg_ref, cb_ref, wout_ref, bout_ref,
                        lg_ref, lb_ref, o_ref, ns_ref, s_ref, u_ref, c_ref, *, bsz, t_len):
    base = HALO_PAD - HALO
    x = x_ref[...]
    h = _dot(x.astype(BF16), win_ref[...]) + bin_ref[...]
    u_ref[...] = h[:, :D_MODEL] * _sigmoid(h[:, D_MODEL:])
    s_ref[:, 0:HALO_PAD, :] = jnp.zeros((bsz, HALO_PAD, D_MODEL), F32)
    s_ref[:, base:HALO_PAD, :] = st_ref[...]
    for b in range(bsz):
        s_ref[b, HALO_PAD:HALO_PAD + t_len, :] = u_ref[b * t_len:(b + 1) * t_len, :]
    acc = jnp.broadcast_to(bdw_ref[...][None], (bsz, t_len, D_MODEL))
    for k in range(CONV_WIDTH):
        acc = acc + wdw_ref[k:k + 1, :][None] * s_ref[:, base + k:base + k + t_len, :]
    for b in range(bsz):
        c_ref[b * t_len:(b + 1) * t_len, :] = acc[b]
    c = _silu(_ln(c_ref[...], cg_ref[...], cb_ref[...]))
    mix = _dot(c.astype(BF16), wout_ref[...]) + bout_ref[...]
    o_ref[...] = _ln(ALPHA * x + mix, lg_ref[...], lb_ref[...])
    ns_ref[...] = s_ref[:, base + t_len:HALO_PAD + t_len, :]


def _conv_layer_sample(x, state, w_in, b_in, w_dw, b_dw, cg, cb, w_out, b_out, lg, lb):
    bsz, t_len, _ = x.shape
    n = bsz * t_len
    assert t_len <= 8
    out, ns = pl.pallas_call(
        functools.partial(_conv_sample_kernel, bsz=bsz, t_len=t_len),
        out_shape=[jax.ShapeDtypeStruct((n, D_MODEL), F32), jax.ShapeDtypeStruct((bsz, HALO, D_MODEL), F32)],
        scratch_shapes=[pltpu.VMEM((bsz, HALO_PAD + 8, D_MODEL), F32), pltpu.VMEM((n, D_MODEL), F32),
                        pltpu.VMEM((n, D_MODEL), F32)],
        compiler_params=pltpu.CompilerParams(vmem_limit_bytes=VMEM_LIMIT),
        name="conv_layer_sample",
    )(x.reshape(n, D_MODEL), state, w_in, b_in, w_dw, b_dw, cg, cb, w_out, b_out, lg, lb)
    return out.reshape(x.shape), ns


def _ffn_kernel(x_ref, wg_ref, wu_ref, wd_ref, g_ref, b_ref, o_ref, acc_ref, xb_ref):
    f = pl.program_id(1)

    @pl.when(f == 0)
    def _():
        acc_ref[...] = jnp.zeros_like(acc_ref)
        xb_ref[...] = x_ref[...].astype(BF16)

    xb = xb_ref[...]
    h = _silu(_dot(xb, wg_ref[...])) * _dot(xb, wu_ref[...])
    acc_ref[...] += _dot(h.astype(BF16), wd_ref[...])

    @pl.when(f == pl.num_programs(1) - 1)
    def _():
        o_ref[...] = _ln(ALPHA * x_ref[...] + acc_ref[...], g_ref[...], b_ref[...])


def _ffn_layer(x, wg, wu, wd, g, b, *, tm=512, n_f=2):
    n = x.shape[0]
    d_ff = wg.shape[1]
    tm = min(tm, n)
    tf = d_ff // n_f
    assert n % tm == 0 and d_ff % n_f == 0 and tf % LANES == 0
    return pl.pallas_call(
        _ffn_kernel,
        grid=(n // tm, n_f),
        in_specs=[
            pl.BlockSpec((tm, D_MODEL), lambda i, f: (i, 0)),
            pl.BlockSpec((D_MODEL, tf), lambda i, f: (0, f)),
            pl.BlockSpec((D_MODEL, tf), lambda i, f: (0, f)),
            pl.BlockSpec((tf, D_MODEL), lambda i, f: (f, 0)),
            _full((1, D_MODEL)), _full((1, D_MODEL)),
        ],
        out_specs=pl.BlockSpec((tm, D_MODEL), lambda i, f: (i, 0)),
        out_shape=jax.ShapeDtypeStruct((n, D_MODEL), F32),
        scratch_shapes=[pltpu.VMEM((tm, D_MODEL), F32), pltpu.VMEM((tm, D_MODEL), BF16)],
        compiler_params=_params(("arbitrary", "arbitrary")),
        name="ffn_dense",
    )(x, wg, wu, wd, g, b)


def _rope_tables(pos):
    half = HEAD_DIM // 2
    inv = 1.0 / (ROPE_THETA ** (jnp.arange(half, dtype=F32) * (2.0 / HEAD_DIM)))
    ang = pos.astype(F32)[:, None] * inv[None, :]
    cos, sin = jnp.cos(ang), jnp.sin(ang)
    return jnp.tile(cos, (1, 4)), jnp.tile(jnp.concatenate([-sin, sin], axis=1), (1, 2))


def _rope(x, cos, sin, first_half):
    half = HEAD_DIM // 2
    out = []
    for c0 in range(0, x.shape[1], LANES):
        slab = x[:, c0:c0 + LANES]
        partner = jnp.where(first_half, pltpu.roll(slab, LANES - half, 1), pltpu.roll(slab, half, 1))
        out.append(slab * cos + partner * sin)
    return jnp.concatenate(out, axis=1)


def _store_dilated(dst_ref, val, s_ref, dil):
    if dil == 1:
        dst_ref[...] = val.astype(BF16)
        return
    d_att = N_HEADS * HEAD_DIM
    rows = val.shape[0] // dil
    for c in range(d_att // LANES):
        s_ref[c] = val[:, c * LANES:(c + 1) * LANES]
    for r in range(dil):
        for c in range(d_att // LANES):
            col = r * d_att + c * LANES
            dst_ref[:, col:col + LANES] = s_ref[c, pl.ds(r, rows, stride=dil), :].astype(BF16)


def _qkv_kernel(x_ref, cos_ref, sin_ref, wkv_ref, wq_ref, kv_ref, *rest, dilated):
    xb = x_ref[...].astype(BF16)
    cos, sin = cos_ref[...], sin_ref[...]
    first_half = (lax.broadcasted_iota(jnp.int32, cos.shape, 1) % HEAD_DIM) < HEAD_DIM // 2
    d_att = N_HEADS * HEAD_DIM
    kv = _dot(xb, wkv_ref[...])
    k = _rope(kv[:, :d_att], cos, sin, first_half)
    v = kv[:, d_att:]
    kv_ref[:, :d_att] = k
    kv_ref[:, d_att:] = v
    q = _rope(_dot(xb, wq_ref[...]), cos, sin, first_half) * (HEAD_DIM ** -0.5)
    if not dilated:
        rest[0][...] = q.astype(BF16)
        return
    s_ref = rest[-1]
    for g, (_, dil) in enumerate(GROUPS):
        q_ref, k_ref, v_ref = rest[3 * g:3 * g + 3]
        _store_dilated(q_ref, q[:, g * d_att:(g + 1) * d_att], s_ref, dil)
        _store_dilated(k_ref, k, s_ref, dil)
        _store_dilated(v_ref, v, s_ref, dil)


def _qkv_layer(x, pos, w_kv, w_q, *, dilated, tm=256):
    n = x.shape[0]
    tm = min(tm, n)
    assert n % tm == 0
    d_att = N_HEADS * HEAD_DIM
    cos, sin = _rope_tables(pos)
    out_specs = [pl.BlockSpec((tm, 2 * d_att), lambda i: (i, 0))]
    out_shape = [jax.ShapeDtypeStruct((n, 2 * d_att), F32)]
    if dilated:
        for _, dil in GROUPS:
            assert tm % (16 * dil) == 0
            out_specs += [pl.BlockSpec((tm // dil, dil * d_att), lambda i: (i, 0))] * 3
            out_shape += [jax.ShapeDtypeStruct((n // dil, dil * d_att), BF16)] * 3
    else:
        out_specs.append(pl.BlockSpec((tm, N_GROUPS * d_att), lambda i: (i, 0)))
        out_shape.append(jax.ShapeDtypeStruct((n, N_GROUPS * d_att), BF16))
    return pl.pallas_call(
        functools.partial(_qkv_kernel, dilated=dilated),
        grid=(n // tm,),
        in_specs=[
            pl.BlockSpec((tm, D_MODEL), lambda i: (i, 0)),
            pl.BlockSpec((tm, LANES), lambda i: (i, 0)),
            pl.BlockSpec((tm, LANES), lambda i: (i, 0)),
            _full((D_MODEL, 2 * d_att)), _full((D_MODEL, N_GROUPS * d_att)),
        ],
        out_specs=out_specs,
        out_shape=out_shape,
        scratch_shapes=[pltpu.VMEM((d_att // LANES, tm, LANES), F32)] if dilated else [],
        compiler_params=_params(("arbitrary",)),
        name="qkv_rope",
    )(x, cos, sin, w_kv, w_q)


ATTN_STEP_ROWS = 512
ATTN_SUB_ROWS = 128


def _attn_prompt_kernel(q_ref, kp_ref, kc_ref, vp_ref, vc_ref, o_ref, lse_ref, kx_ref, vx_ref, *, rows, sub):
    blk = pl.program_id(1)
    back = KEYS_PER_GROUP
    span = back + sub
    kx_ref[0:back, :] = kp_ref[...]
    kx_ref[back:back + rows, :] = kc_ref[...]
    vx_ref[0:back, :] = vp_ref[...]
    vx_ref[back:back + rows, :] = vc_ref[...]
    a = lax.broadcasted_iota(jnp.int32, (sub, span), 0)
    c = lax.broadcasted_iota(jnp.int32, (sub, span), 1)
    band = (c >= a) & (c <= a + back)
    lane = lax.broadcasted_iota(jnp.int32, (sub, LANES), 1)
    low = lane < HEAD_DIM

    for u in range(rows // sub):
        r0 = u * sub
        valid = band & ((blk > 0) | (c + r0 >= back)) if r0 < back else band
        lse_all = jnp.zeros((sub, LANES), F32)
        for p in range(N_HEADS // 2):
            sl = slice(p * LANES, (p + 1) * LANES)
            qp = q_ref[pl.ds(r0, sub), sl]
            kk = kx_ref[pl.ds(r0, span), sl]
            vv = vx_ref[pl.ds(r0, span), sl]
            outs = []
            for j in range(2):
                qm = jnp.where(low if j == 0 else jnp.logical_not(low), qp, jnp.zeros_like(qp))
                s = lax.dot_general(qm, kk, (((1,), (1,)), ((), ())), preferred_element_type=F32)
                s = jnp.where(valid, s, NEG)
                m = jnp.max(s, axis=1, keepdims=True)
                e = jnp.exp(s - m)
                den = jnp.sum(e, axis=1, keepdims=True)
                outs.append(_dot(e.astype(BF16), vv) / den)
                lse_all = jnp.where(lane == 2 * p + j, m + jnp.log(den), lse_all)
            o_ref[pl.ds(r0, sub), sl] = jnp.where(low, outs[0], outs[1]).astype(BF16)
        lse_ref[0, pl.ds(r0, sub), :] = lse_all


def _attn_prompt_group(qg, kg, vg, g):
    dil = GROUPS[g][1]
    back = KEYS_PER_GROUP
    assert GROUPS[g][0] // dil == back
    d_att = N_HEADS * HEAD_DIM
    n_rows = kg.shape[0]
    rows = min(ATTN_STEP_ROWS, n_rows)
    sub = ATTN_SUB_ROWS
    assert kg.shape[1] == dil * d_att and n_rows % rows == 0 and rows % back == 0 and rows % sub == 0
    cur = pl.BlockSpec((rows, d_att), lambda r, b: (b, r))
    prev = pl.BlockSpec((back, d_att), lambda r, b: (jnp.maximum(b * (rows // back) - 1, 0), r))
    return pl.pallas_call(
        functools.partial(_attn_prompt_kernel, rows=rows, sub=sub),
        grid=(dil, n_rows // rows),
        in_specs=[cur, prev, cur, prev, cur],
        out_specs=[cur, pl.BlockSpec((1, rows, LANES), lambda r, b: (r, b, 0))],
        out_shape=[jax.ShapeDtypeStruct((n_rows, dil * d_att), BF16),
                   jax.ShapeDtypeStruct((dil, n_rows, LANES), F32)],
        scratch_shapes=[pltpu.VMEM((back + rows, d_att), BF16), pltpu.VMEM((back + rows, d_att), BF16)],
        compiler_params=_params(("arbitrary", "arbitrary")),
        name=f"attn_prompt_g{g}",
    )(qg, kg, kg, vg, vg)


SAMPLE_ROWS = 16


def _sample_valid(t_len, past):
    key = np.arange(past + LANES)
    valid = np.zeros((SAMPLE_ROWS, past + LANES), np.float32)
    for r in range(N_GROUPS * t_len):
        g, i = divmod(r, t_len)
        window, dil = GROUPS[g]
        dist = past + i - key
        ok = (dist >= 0) & (dist <= window) & (dist % dil == 0) & (key < past + t_len)
        assert ok.sum() == window // dil + 1
        valid[r] = ok
    return valid[:, :past], valid[:, past:]


def _group_reduce(x, op):
    quarter = SAMPLE_ROWS // 4
    return op(op(x, pltpu.roll(x, quarter, 0)), op(pltpu.roll(x, 2 * quarter, 0), pltpu.roll(x, 3 * quarter, 0)))


def _attn_sample_kernel(q_ref, kvn_ref, vc_ref, vn_ref, cache_ref, o_ref, *, t_len):
    d_att = N_HEADS * HEAD_DIM
    q = q_ref[0]
    q_rows = jnp.concatenate([q[:, g * d_att:(g + 1) * d_att] for g in range(N_GROUPS)]
                             + [jnp.zeros((SAMPLE_ROWS - N_GROUPS * t_len, d_att), BF16)], axis=0)
    kvn = jnp.concatenate([kvn_ref[0], jnp.zeros((LANES - t_len, 2 * d_att), F32)], axis=0).astype(BF16)
    valid_c, valid_n = vc_ref[...] > 0.0, vn_ref[...] > 0.0
    ones = jnp.ones((SAMPLE_ROWS, LANES), F32)
    outs = []
    for h in range(N_HEADS):
        sl = slice(h * HEAD_DIM, (h + 1) * HEAD_DIM)
        qh = q_rows[:, sl]
        kt = cache_ref[0, 0, h].astype(BF16)
        vt = cache_ref[0, 1, h].astype(BF16)
        s_c = jnp.where(valid_c, _dot(qh, kt), NEG)
        s_n = jnp.where(valid_n, lax.dot_general(qh, kvn[:, sl], (((1,), (1,)), ((), ())),
                                                 preferred_element_type=F32), NEG)
        m = jnp.maximum(jnp.max(s_c, axis=1, keepdims=True), jnp.max(s_n, axis=1, keepdims=True))
        m = _group_reduce(m * ones, jnp.maximum)[:, 0:1]
        e_c, e_n = jnp.exp(s_c - m), jnp.exp(s_n - m)
        den = jnp.sum(e_c, axis=1, keepdims=True) + jnp.sum(e_n, axis=1, keepdims=True)
        den = _group_reduce(den * ones, jnp.add)[:, 0:1]
        o = lax.dot_general((e_c / den).astype(BF16), vt, (((1,), (1,)), ((), ())), preferred_element_type=F32)
        o = o + _dot((e_n / den).astype(BF16), kvn[:, d_att + h * HEAD_DIM:d_att + (h + 1) * HEAD_DIM])
        out = o[0:t_len]
        for g in range(1, N_GROUPS):
            out = out + o[g * t_len:(g + 1) * t_len]
        outs.append(out)
    o_ref[0] = jnp.concatenate(outs, axis=1)


def _attn_sample(q, kv_new, cache_t):
    bsz, t_len, _ = q.shape
    past = cache_t.shape[-1]
    d_att = N_HEADS * HEAD_DIM
    assert N_GROUPS * t_len <= SAMPLE_ROWS and SAMPLE_ROWS % 4 == 0 and SAMPLE_ROWS // 4 == t_len
    assert PAST_LEN >= past and past >= max(w for w, _ in GROUPS)
    valid_c, valid_n = _sample_valid(t_len, past)
    return pl.pallas_call(
        functools.partial(_attn_sample_kernel, t_len=t_len),
        grid=(bsz,),
        in_specs=[
            pl.BlockSpec((1, t_len, N_GROUPS * d_att), lambda b: (b, 0, 0)),
            pl.BlockSpec((1, t_len, 2 * d_att), lambda b: (b, 0, 0)),
            _full(valid_c.shape), _full(valid_n.shape),
            pl.BlockSpec((1, 2, N_HEADS, HEAD_DIM, past), lambda b: (b, 0, 0, 0, 0)),
        ],
        out_specs=pl.BlockSpec((1, t_len, d_att), lambda b: (b, 0, 0)),
        out_shape=jax.ShapeDtypeStruct((bsz, t_len, d_att), F32),
        compiler_params=_params(("arbitrary",)),
        name="attn_sample",
    )(q, kv_new, jnp.asarray(valid_c), jnp.asarray(valid_n), cache_t)


def _store_token_tiles(ref, x):
    n = x.shape[0]
    for s in range(TILE_ROWS):
        ref[pl.ds(s, n, stride=TILE_ROWS), :] = x[:, s * LANES:(s + 1) * LANES]


def _load_token_slab(ref, n, s):
    return ref[pl.ds(s, n, stride=TILE_ROWS), :]


def _token_rows(ref, t):
    return ref.at[pl.ds(pl.multiple_of(t * TILE_ROWS, TILE_ROWS), TILE_ROWS), :]


def _split3(w):
    hi = w.astype(BF16)
    r1 = w - hi.astype(F32)
    mid = r1.astype(BF16)
    lo = (r1 - mid.astype(F32)).astype(BF16)
    return hi, mid, lo


def _oproj_router_tail(att, x_ref, wo_ref, g_ref, b_ref, wrh_ref, wrl_ref, x3t_ref, route_ref, cnt_ref):
    tm = att.shape[0]

    @pl.when(pl.program_id(0) == 0)
    def _():
        cnt_ref[...] = jnp.zeros_like(cnt_ref)

    x3 = _ln(ALPHA * x_ref[...] + _dot(att.astype(BF16), wo_ref[...]), g_ref[...], b_ref[...])
    _store_token_tiles(x3t_ref, x3)
    xh = x3.astype(BF16)
    xl = (x3 - xh.astype(F32)).astype(BF16)
    logits = _dot(xh, wrh_ref[...]) + (_dot(xh, wrl_ref[...]) + _dot(xl, wrh_ref[...]))
    lane = lax.broadcasted_iota(jnp.int32, logits.shape, 1)
    logits = jnp.where(lane < N_EXPERTS, logits, NEG)
    v1 = jnp.max(logits, axis=1, keepdims=True)
    i1 = jnp.min(jnp.where(logits == v1, lane, LANES), axis=1, keepdims=True)
    rest = jnp.where(lane == i1, NEG, logits)
    v2 = jnp.max(rest, axis=1, keepdims=True)
    i2 = jnp.min(jnp.where(rest == v2, lane, LANES), axis=1, keepdims=True)
    e2 = jnp.exp(v2 - v1)
    den = 1.0 + e2
    sel = jnp.where((lane == i1) | (lane == i2), 1.0, 0.0)
    tri = jnp.where(lax.broadcasted_iota(jnp.int32, (tm, tm), 1) < lax.broadcasted_iota(jnp.int32, (tm, tm), 0),
                    1.0, 0.0).astype(BF16)
    before = _dot(tri, sel.astype(BF16)) + cnt_ref[0:1, :]
    rank1 = jnp.sum(jnp.where(lane == i1, before, 0.0), axis=1, keepdims=True)
    rank2 = jnp.sum(jnp.where(lane == i2, before, 0.0), axis=1, keepdims=True)
    cnt_ref[0:1, :] = cnt_ref[0:1, :] + jnp.sum(sel, axis=0, keepdims=True)
    fields = (i1.astype(F32), i2.astype(F32), 1.0 / den, e2 / den, rank1, rank2)
    route = jnp.zeros(logits.shape, F32)
    for k, val in enumerate(fields):
        route = jnp.where(lane == k, val, route)
    route_ref[...] = route


def _load_dilated(o_ref, lse_ref, o_scr, l_scr, dil):
    if dil == 1:
        return o_ref[...].astype(F32), lse_ref[0]
    d_att = N_HEADS * HEAD_DIM
    rows = o_ref.shape[0]
    n_slabs = d_att // LANES
    for r in range(dil):
        for c in range(n_slabs):
            col = r * d_att + c * LANES
            o_scr[c, pl.ds(r, rows, stride=dil), :] = o_ref[:, col:col + LANES].astype(F32)
        l_scr[pl.ds(r, rows, stride=dil), :] = lse_ref[r]
    return jnp.concatenate([o_scr[c] for c in range(n_slabs)], axis=1), l_scr[...]


def _oproj_prompt_kernel(o0_ref, l0_ref, o1_ref, l1_ref, o2_ref, l2_ref, ex_ref, x_ref, wo_ref, g_ref, b_ref,
                         wrh_ref, wrl_ref, x3t_ref, route_ref, cnt_ref, *scr):
    parts = [_load_dilated(o_ref, l_ref, scr[2 * g], scr[2 * g + 1], GROUPS[g][1])
             for g, (o_ref, l_ref) in enumerate(((o0_ref, l0_ref), (o1_ref, l1_ref), (o2_ref, l2_ref)))]
    lse = [l for _, l in parts]
    m = jnp.maximum(jnp.maximum(lse[0], lse[1]), lse[2])
    w = [jnp.exp(l - m) for l in lse]
    den = w[0] + w[1] + w[2]
    att = None
    for g, (o, _) in enumerate(parts):
        hi, mid, lo = _split3(w[g] / den)
        wexp = _dot(hi, ex_ref[...]) + (_dot(mid, ex_ref[...]) + _dot(lo, ex_ref[...]))
        term = wexp * o
        att = term if att is None else att + term
    _oproj_router_tail(att, x_ref, wo_ref, g_ref, b_ref, wrh_ref, wrl_ref, x3t_ref, route_ref, cnt_ref)


def _oproj_sample_kernel(att_ref, x_ref, wo_ref, g_ref, b_ref, wrh_ref, wrl_ref, x3t_ref, route_ref, cnt_ref):
    _oproj_router_tail(att_ref[...], x_ref, wo_ref, g_ref, b_ref, wrh_ref, wrl_ref, x3t_ref, route_ref, cnt_ref)


def _router_split(w_router):
    wr = jnp.zeros((D_MODEL, LANES), F32).at[:, :N_EXPERTS].set(w_router)
    hi = wr.astype(BF16)
    return hi, (wr - hi.astype(F32)).astype(BF16)


def _oproj_router(att_parts, x, w_o, g, b, w_router, *, tm=256):
    n = x.shape[0]
    tm = min(tm, n)
    assert n % tm == 0
    d_att = N_HEADS * HEAD_DIM
    wrh, wrl = _router_split(w_router)
    row_blk = lambda w: pl.BlockSpec((tm, w), lambda i: (i, 0))
    tail_specs = [row_blk(D_MODEL), _full((d_att, D_MODEL)), _full((1, D_MODEL)), _full((1, D_MODEL)),
                  _full((D_MODEL, LANES)), _full((D_MODEL, LANES))]
    tail_args = (x, w_o, g, b, wrh, wrl)
    scratch = []
    if len(att_parts) == 1:
        kern, specs, args = _oproj_sample_kernel, [row_blk(d_att)], tuple(att_parts)
    else:
        expand = (np.arange(LANES)[:, None] == np.arange(d_att)[None, :] // HEAD_DIM).astype(np.float32)
        kern, specs, args = _oproj_prompt_kernel, [], []
        for (_, dil), (o, lse) in zip(GROUPS, att_parts):
            specs += [pl.BlockSpec((tm // dil, dil * d_att), lambda i: (i, 0)),
                      pl.BlockSpec((dil, tm // dil, LANES), lambda i: (0, i, 0))]
            args += [o, lse]
            scratch += [pltpu.VMEM((d_att // LANES, tm, LANES), F32), pltpu.VMEM((tm, LANES), F32)]
        specs.append(_full((LANES, d_att)))
        args.append(jnp.asarray(expand, BF16))
    return pl.pallas_call(
        kern,
        grid=(n // tm,),
        in_specs=specs + tail_specs,
        out_specs=[pl.BlockSpec((tm * TILE_ROWS, LANES), lambda i: (i, 0)), row_blk(LANES), _full((TILE_ROWS, LANES))],
        out_shape=[jax.ShapeDtypeStruct((n * TILE_ROWS, LANES), F32), jax.ShapeDtypeStruct((n, LANES), F32),
                   jax.ShapeDtypeStruct((TILE_ROWS, LANES), F32)],
        scratch_shapes=scratch,
        compiler_params=_params(("arbitrary",)),
        name="oproj_router",
    )(*args, *tail_args)


def _route_plan(routes, counts, tile, n_tiles):
    cnt = [c[0, :N_EXPERTS].astype(jnp.int32) for c in counts]
    total = sum(cnt)
    padded = (total + tile - 1) // tile * tile
    ends = jnp.cumsum(padded)
    offs = ends - padded
    fill = jnp.concatenate([offs + total, ends]).astype(jnp.int32)
    pos = []
    for route, c in zip(routes, cnt):
        e = route[:, 0:2].astype(jnp.int32)
        pos.append((offs[e] + route[:, 4:6].astype(jnp.int32)).reshape(-1))
        offs = offs + c
    n_live = (ends[-1] // tile).astype(jnp.int32)
    tile_id = jnp.minimum(jnp.arange(n_tiles, dtype=jnp.int32), n_live - 1)
    tile_expert = jnp.sum((ends // tile)[None, :] <= tile_id[:, None], axis=1).astype(jnp.int32)
    return pos, tile_expert, n_live.reshape(1), fill


def _dispatch_kernel(pos_ref, fill_ref, xa_ref, xb_ref, xs_ref, z_ref, sem, *, tm, steps_a, n_b, tile, n_tiles):
    i = pl.program_id(0)

    def scatter(src_ref, count, first_token):
        def row_copy(r, s):
            dst = _token_rows(xs_ref, pos_ref[(first_token + r) * 2 + s])
            return pltpu.make_async_copy(_token_rows(src_ref, r), dst, sem)

        def issue(r, c):
            row_copy(r, 0).start(priority=0)
            row_copy(r, 1).start(priority=1)
            return c

        def drain(r, c):
            row_copy(r, 0).wait()
            row_copy(r, 1).wait()
            return c

        lax.fori_loop(0, count, issue, 0)
        lax.fori_loop(0, count, drain, 0)

    @pl.when(i < steps_a)
    def _():
        scatter(xa_ref, tm, i * tm)

    @pl.when(i == steps_a)
    def _():
        scatter(xb_ref, n_b, steps_a * tm)
        z_ref[...] = jnp.zeros_like(z_ref)
        zero_row = lambda r: pltpu.make_async_copy(z_ref.at[0:TILE_ROWS, :], _token_rows(xs_ref, r), sem)
        tile_rows = tile * TILE_ROWS
        zero_tile = lambda j: pltpu.make_async_copy(
            z_ref, xs_ref.at[pl.ds(pl.multiple_of(j * tile_rows, tile_rows), tile_rows), :], sem)

        def loop(lo, hi, copy):
            def issue(r, c):
                copy(r).start()
                return c

            def drain(r, c):
                copy(r).wait()
                return c

            lax.fori_loop(lo, hi, issue, 0)
            lax.fori_loop(lo, hi, drain, 0)

        for e in range(N_EXPERTS):
            loop(fill_ref[e], fill_ref[N_EXPERTS + e], zero_row)
        loop(fill_ref[2 * N_EXPERTS - 1] // tile, n_tiles, zero_tile)


def _dispatch(pos, fill_rng, x3t_a, x3t_b, n_tiles, tile, *, tm=256):
    n_a, n_b = x3t_a.shape[0] // TILE_ROWS, x3t_b.shape[0] // TILE_ROWS
    tm = min(tm, n_a)
    assert n_a % tm == 0
    steps_a = n_a // tm
    return pl.pallas_call(
        functools.partial(_dispatch_kernel, tm=tm, steps_a=steps_a, n_b=n_b, tile=tile, n_tiles=n_tiles),
        grid_spec=pltpu.PrefetchScalarGridSpec(
            num_scalar_prefetch=2, grid=(steps_a + 1,),
            in_specs=[pl.BlockSpec((tm * TILE_ROWS, LANES), lambda i, *_: (jnp.minimum(i, steps_a - 1), 0)),
                      pl.BlockSpec((n_b * TILE_ROWS, LANES), lambda i, *_: (0, 0))],
            out_specs=pl.BlockSpec(memory_space=pl.ANY),
            scratch_shapes=[pltpu.VMEM((tile * TILE_ROWS, LANES), F32), pltpu.SemaphoreType.DMA(())]),
        out_shape=jax.ShapeDtypeStruct((n_tiles * tile * TILE_ROWS, LANES), F32),
        compiler_params=_params(("arbitrary",)),
        name="moe_dispatch",
    )(pos, fill_rng, x3t_a, x3t_b)


def _moe_ffn_kernel(te_ref, nl_ref, xs_ref, wg_ref, wu_ref, wd_ref, ys_ref, acc_ref, xb_ref, *, tile):
    j, f = pl.program_id(0), pl.program_id(1)

    @pl.when(j < nl_ref[0])
    def _():
        @pl.when(f == 0)
        def _():
            for s in range(TILE_ROWS):
                xb_ref[:, s * LANES:(s + 1) * LANES] = _load_token_slab(xs_ref, tile, s).astype(BF16)

        xb = xb_ref[...]
        h = _silu(_dot(xb, wg_ref[...])) * _dot(xb, wu_ref[...])
        y = _dot(h.astype(BF16), wd_ref[...])

        @pl.when(f == 0)
        def _():
            acc_ref[...] = y

        @pl.when(f > 0)
        def _():
            acc_ref[...] += y

        @pl.when(f == pl.num_programs(1) - 1)
        def _():
            _store_token_tiles(ys_ref, acc_ref[...])

    @pl.when((j >= nl_ref[0]) & (f == pl.num_programs(1) - 1))
    def _():
        ys_ref[...] = jnp.zeros_like(ys_ref)


def _moe_ffn(xs, tile_expert, n_live, wg, wu, wd, *, tile, n_f=2):
    n_tiles = tile_expert.shape[0]
    d_ff = wg.shape[2]
    tf = d_ff // n_f
    assert d_ff % n_f == 0 and tf % LANES == 0 and xs.shape[0] == n_tiles * tile * TILE_ROWS
    row_map = lambda j, f, te, nl: (jnp.minimum(j, nl[0] - 1), 0)
    f_idx = lambda j, f, nl: jnp.where(j < nl[0], f, n_f - 1)
    return pl.pallas_call(
        functools.partial(_moe_ffn_kernel, tile=tile),
        grid_spec=pltpu.PrefetchScalarGridSpec(
            num_scalar_prefetch=2, grid=(n_tiles, n_f),
            in_specs=[
                pl.BlockSpec((tile * TILE_ROWS, LANES), row_map),
                pl.BlockSpec((None, D_MODEL, tf), lambda j, f, te, nl: (te[j], 0, f_idx(j, f, nl))),
                pl.BlockSpec((None, D_MODEL, tf), lambda j, f, te, nl: (te[j], 0, f_idx(j, f, nl))),
                pl.BlockSpec((None, tf, D_MODEL), lambda j, f, te, nl: (te[j], f_idx(j, f, nl), 0)),
            ],
            out_specs=pl.BlockSpec((tile * TILE_ROWS, LANES), lambda j, f, te, nl: (j, 0)),
            scratch_shapes=[pltpu.VMEM((tile, D_MODEL), F32), pltpu.VMEM((tile, D_MODEL), BF16)]),
        out_shape=jax.ShapeDtypeStruct(xs.shape, F32),
        compiler_params=_params(("arbitrary", "arbitrary")),
        name="moe_ffn",
    )(tile_expert, n_live, xs, wg, wu, wd)


def _moe_combine_kernel(pos_ref, route_ref, x3t_ref, ys_ref, g_ref, b_ref, o_ref, y1_ref, y2_ref, pre_ref, sem, *, tm):
    i = pl.program_id(0)
    bufs = (y1_ref, y2_ref)

    def row_copy(r, s):
        return pltpu.make_async_copy(_token_rows(ys_ref, pos_ref[(i * tm + r) * 2 + s]), _token_rows(bufs[s], r), sem)

    def issue(r, c):
        row_copy(r, 0).start(priority=0)
        row_copy(r, 1).start(priority=1)
        return c

    def drain(r, c):
        row_copy(r, 0).wait()
        row_copy(r, 1).wait()
        return c

    lax.fori_loop(0, tm, issue, 0)
    lax.fori_loop(0, tm, drain, 0)
    route = route_ref[...]
    g1, g2 = route[:, 2:3], route[:, 3:4]
    for s in range(TILE_ROWS):
        y = g1 * _load_token_slab(y1_ref, tm, s) + g2 * _load_token_slab(y2_ref, tm, s)
        pre_ref[:, s * LANES:(s + 1) * LANES] = ALPHA * _load_token_slab(x3t_ref, tm, s) + y
    o_ref[...] = _ln(pre_ref[...], g_ref[...], b_ref[...])


def _moe_combine(pos, route, x3t, ys, g, b, *, tm=256):
    n = route.shape[0]
    tm = min(tm, n)
    assert n % tm == 0
    tok_buf = pltpu.VMEM((tm * TILE_ROWS, LANES), F32)
    return pl.pallas_call(
        functools.partial(_moe_combine_kernel, tm=tm),
        grid_spec=pltpu.PrefetchScalarGridSpec(
            num_scalar_prefetch=1, grid=(n // tm,),
            in_specs=[
                pl.BlockSpec((tm, LANES), lambda i, *_: (i, 0)),
                pl.BlockSpec((tm * TILE_ROWS, LANES), lambda i, *_: (i, 0)),
                pl.BlockSpec(memory_space=pl.ANY),
                pl.BlockSpec((1, D_MODEL), lambda i, *_: (0, 0)),
                pl.BlockSpec((1, D_MODEL), lambda i, *_: (0, 0)),
            ],
            out_specs=pl.BlockSpec((tm, D_MODEL), lambda i, *_: (i, 0)),
            scratch_shapes=[tok_buf, tok_buf, pltpu.VMEM((tm, D_MODEL), F32), pltpu.SemaphoreType.DMA(())]),
        out_shape=jax.ShapeDtypeStruct((n, D_MODEL), F32),
        compiler_params=_params(("arbitrary",)),
        name="moe_combine",
    )(pos, route, x3t, ys, g, b)


def kernel(x_prompt, x_sample, cache_kv, state_conv, conv_w_in, conv_b_in, conv_w_dw, conv_b_dw, conv_ln_g,
           conv_ln_b, conv_w_out, conv_b_out, ffn_w_gate, ffn_w_up, ffn_w_down, w_kv, attn_w_q, attn_w_o,
           moe_w_router, moe_w_gate, moe_w_up, moe_w_down, ln_g, ln_b):
    bp, t_p, _ = x_prompt.shape
    bs, t_s, _ = x_sample.shape
    assert bp == 1 and conv_w_in.shape[0] == 1 and attn_w_q.shape[0] == 1
    d_att = N_HEADS * HEAD_DIM
    row = lambda a: a.reshape(1, -1)
    bf = lambda a: a.astype(BF16)

    conv_w = (bf(conv_w_in[0]), row(conv_b_in[0]), conv_w_dw[0], row(conv_b_dw[0]), row(conv_ln_g[0]),
              row(conv_ln_b[0]), bf(conv_w_out[0]), row(conv_b_out[0]), row(ln_g[0, 0]), row(ln_b[0, 0]))
    ffn_w = (bf(ffn_w_gate[0]), bf(ffn_w_up[0]), bf(ffn_w_down[0]), row(ln_g[0, 1]), row(ln_b[0, 1]))
    wkv_b, wq_b, wo_b = bf(w_kv), bf(attn_w_q[0]), bf(attn_w_o[0])
    ln10 = (row(ln_g[1, 0]), row(ln_b[1, 0]))
    ln11 = (row(ln_g[1, 1]), row(ln_b[1, 1]))

    x1, conv_p = _conv_layer_prompt(x_prompt, jnp.zeros((bp, HALO, D_MODEL), F32), *conv_w)
    x2 = _ffn_layer(x1.reshape(t_p, D_MODEL), *ffn_w)
    kv_p, *qkv_views = _qkv_layer(x2, jnp.arange(t_p, dtype=jnp.int32), wkv_b, wq_b, dilated=True)
    parts = [_attn_prompt_group(*qkv_views[3 * g:3 * g + 3], g) for g in range(N_GROUPS)]
    x3t_p, route_p, cnt_p = _oproj_router(parts, x2, wo_b, *ln10, moe_w_router[0])
    n_keep = min(max(w for w, _ in GROUPS), t_p)
    kv_prompt = kv_p[t_p - n_keep:].reshape(bp, n_keep, 2, N_HEADS, HEAD_DIM)

    n_s = bs * t_s
    x1s, conv_s = _conv_layer_sample(x_sample, state_conv[0], *conv_w)
    x2s = _ffn_layer(x1s.reshape(n_s, D_MODEL), *ffn_w)
    pos_s = PAST_LEN + jnp.tile(jnp.arange(t_s, dtype=jnp.int32), bs)
    kv_s, q_s = _qkv_layer(x2s, pos_s, wkv_b, wq_b, dilated=False)
    att_s = _attn_sample(q_s.reshape(bs, t_s, N_GROUPS * d_att), kv_s.reshape(bs, t_s, 2 * d_att),
                         jnp.transpose(cache_kv, (0, 2, 3, 4, 1)))
    x3t_s, route_s, cnt_s = _oproj_router((att_s.reshape(n_s, d_att),), x2s, wo_b, *ln10, moe_w_router[0])
    kv_sample = kv_s.reshape(bs, t_s, 2, N_HEADS, HEAD_DIM)

    n_slots = 2 * (t_p + n_s)
    n_tiles = (n_slots + N_EXPERTS * (MOE_TILE - 1)) // MOE_TILE
    (pos_p, pos_sm), tile_expert, n_live, fill_rng = _route_plan((route_p, route_s), (cnt_p, cnt_s), MOE_TILE, n_tiles)
    xs = _dispatch(jnp.concatenate([pos_p, pos_sm]), fill_rng, x3t_p, x3t_s, n_tiles, MOE_TILE)
    ys = _moe_ffn(xs, tile_expert, n_live, bf(moe_w_gate[0]), bf(moe_w_up[0]), bf(moe_w_down[0]), tile=MOE_TILE)
    y_prompt = _moe_combine(pos_p, route_p, x3t_p, ys, *ln11).reshape(bp, t_p, D_MODEL)
    y_sample = _moe_combine(pos_sm, route_s, x3t_s, ys, *ln11).reshape(bs, t_s, D_MODEL)

    return (y_prompt, y_sample, conv_p[None], conv_s[None], kv_prompt, kv_sample)
```

```python
import functools

import numpy as np
import jax
import jax.numpy as jnp
from jax import lax
from jax.experimental import pallas as pl
from jax.experimental.pallas import tpu as pltpu

F32 = jnp.float32
BF16 = jnp.bfloat16

D_MODEL = 1024
N_HEADS = 16
HEAD_DIM = 64
GROUPS = ((128, 1), (512, 4), (2048, 16))
N_GROUPS = len(GROUPS)
KEYS_PER_GROUP = 128
CONV_WIDTH = 31
HALO = CONV_WIDTH - 1
N_EXPERTS = 8
ROPE_THETA = 10000.0
LN_EPS = 1e-5
DEPTH = 2
ALPHA = (2 * DEPTH) ** 0.25
PAST_LEN = 16384
NEG = -1e30

LANES = 128
HALO_PAD = 32
TILE_ROWS = 8
MOE_TILE = 512
VMEM_LIMIT = 56 * 1024 * 1024


def _params(sem, vmem=VMEM_LIMIT):
    return pltpu.CompilerParams(dimension_semantics=sem, vmem_limit_bytes=vmem)


def _dot(a, b):
    return jnp.dot(a, b, preferred_element_type=F32)


def _sigmoid(x):
    return 1.0 / (1.0 + jnp.exp(-x))


def _silu(x):
    return x * _sigmoid(x)


def _ln(x, g, b):
    xc = x - jnp.mean(x, axis=-1, keepdims=True)
    var = jnp.mean(xc * xc, axis=-1, keepdims=True)
    return xc * lax.rsqrt(var + LN_EPS) * g + b


def _full(shape):
    return pl.BlockSpec(shape, lambda *_: (0,) * len(shape))


def _conv_taps(s_ref, wdw_ref, bdw_ref, c_ref, sh_ref, n_rows, row_chunk):
    base = HALO_PAD - HALO
    phases = [[k for k in range(CONV_WIDTH) if (base + k) % TILE_ROWS == ph] for ph in range(TILE_ROWS)]

    def chunk(rc, carry):
        r0 = pl.multiple_of(rc * row_chunk, row_chunk)
        win = s_ref.at[pl.ds(r0, row_chunk + HALO_PAD), :]
        for c0 in range(0, D_MODEL, LANES):
            for ph, taps in enumerate(phases):
                n_ext = row_chunk + taps[-1] - taps[0]
                sh_ref[ph, 0:n_ext, :] = win[base + taps[0]:base + taps[0] + n_ext, c0:c0 + LANES]
            acc = jnp.broadcast_to(bdw_ref[:, c0:c0 + LANES], (row_chunk, LANES))
            for ph, taps in enumerate(phases):
                for k in taps:
                    acc = acc + wdw_ref[k:k + 1, c0:c0 + LANES] * sh_ref[ph, k - taps[0]:k - taps[0] + row_chunk, :]
            c_ref[pl.ds(r0, row_chunk), c0:c0 + LANES] = acc
        return carry

    lax.fori_loop(0, n_rows // row_chunk, chunk, 0)


def _conv_prompt_kernel(x_ref, st_ref, win_ref, bin_ref, wdw_ref, bdw_ref, cg_ref, cb_ref, wout_ref, bout_ref,
                        lg_ref, lb_ref, o_ref, ns_ref, s_ref, c_ref, sh_ref, *, tm, row_chunk):
    base = HALO_PAD - HALO

    @pl.when(pl.program_id(1) == 0)
    def _():
        s_ref[0:HALO_PAD, :] = jnp.zeros((HALO_PAD, D_MODEL), F32)
        s_ref[base:HALO_PAD, :] = st_ref[0]

    x = x_ref[0]
    h = _dot(x.astype(BF16), win_ref[...]) + bin_ref[...]
    s_ref[HALO_PAD:HALO_PAD + tm, :] = h[:, :D_MODEL] * _sigmoid(h[:, D_MODEL:])
    _conv_taps(s_ref, wdw_ref, bdw_ref, c_ref, sh_ref, tm, row_chunk)
    c = _silu(_ln(c_ref[...], cg_ref[...], cb_ref[...]))
    mix = _dot(c.astype(BF16), wout_ref[...]) + bout_ref[...]
    o_ref[0] = _ln(ALPHA * x + mix, lg_ref[...], lb_ref[...])
    tail = s_ref[pl.ds(tm + base, HALO), :]
    s_ref[base:HALO_PAD, :] = tail
    ns_ref[0] = tail


def _conv_layer_prompt(x, state, w_in, b_in, w_dw, b_dw, cg, cb, w_out, b_out, lg, lb, *, tm=256):
    bsz, t_len, _ = x.shape
    tm = min(tm, t_len)
    assert t_len % tm == 0 and tm >= HALO_PAD and tm % 8 == 0
    row_chunk = 64 if tm % 64 == 0 else 8
    row = lambda n: _full((1, n))
    return pl.pallas_call(
        functools.partial(_conv_prompt_kernel, tm=tm, row_chunk=row_chunk),
        grid=(bsz, t_len // tm),
        in_specs=[
            pl.BlockSpec((1, tm, D_MODEL), lambda b, i: (b, i, 0)),
            pl.BlockSpec((1, HALO, D_MODEL), lambda b, i: (b, 0, 0)),
            _full((D_MODEL, 2 * D_MODEL)), row(2 * D_MODEL),
            _full((CONV_WIDTH, D_MODEL)), row(D_MODEL), row(D_MODEL), row(D_MODEL),
            _full((D_MODEL, D_MODEL)), row(D_MODEL), row(D_MODEL), row(D_MODEL),
        ],
        out_specs=[
            pl.BlockSpec((1, tm, D_MODEL), lambda b, i: (b, i, 0)),
            pl.BlockSpec((1, HALO, D_MODEL), lambda b, i: (b, 0, 0)),
        ],
        out_shape=[jax.ShapeDtypeStruct(x.shape, F32), jax.ShapeDtypeStruct((bsz, HALO, D_MODEL), F32)],
        scratch_shapes=[pltpu.VMEM((HALO_PAD + tm, D_MODEL), F32), pltpu.VMEM((tm, D_MODEL), F32),
                        pltpu.VMEM((TILE_ROWS, row_chunk + HALO_PAD, LANES), F32)],
        compiler_params=_params(("arbitrary", "arbitrary")),
        name="conv_layer_prompt",
    )(x, state, w_in, b_in, w_dw, b_dw, cg, cb, w_out, b_out, lg, lb)


def _conv_sample_kernel(x_ref, st_ref, win_ref, bin_ref, wdw_ref, bdw_ref, cg_ref, cb_ref, wout_ref, bout_ref,
                        lg_ref, lb_ref, o_ref, ns_ref, s_ref, u_ref, c_ref, *, bsz, t_len):
    base = HALO_PAD - HALO
    x = x_ref[...]
    h = _dot(x.astype(BF16), win_ref[...]) + bin_ref[...]
    u_ref[...] = h[:, :D_MODEL] * _sigmoid(h[:, D_MODEL:])
    s_ref[:, 0:HALO_PAD, :] = jnp.zeros((bsz, HALO_PAD, D_MODEL), F32)
    s_ref[:, base:HALO_PAD, :] = st_ref[...]
    for b in range(bsz):
        s_ref[b, HALO_PAD:HALO_PAD + t_len, :] = u_ref[b * t_len:(b + 1) * t_len, :]
    acc = jnp.broadcast_to(bdw_ref[...][None], (bsz, t_len, D_MODEL))
    for k in range(CONV_WIDTH):
        acc = acc + wdw_ref[k:k + 1, :][None] * s_ref[:, base + k:base + k + t_len, :]
    for b in range(bsz):
        c_ref[b * t_len:(b + 1) * t_len, :] = acc[b]
    c = _silu(_ln(c_ref[...], cg_ref[...], cb_ref[...]))
    mix = _dot(c.astype(BF16), wout_ref[...]) + bout_ref[...]
    o_ref[...] = _ln(ALPHA * x + mix, lg_ref[...], lb_ref[...])
    ns_ref[...] = s_ref[:, base + t_len:HALO_PAD + t_len, :]


def _conv_layer_sample(x, state, w_in, b_in, w_dw, b_dw, cg, cb, w_out, b_out, lg, lb):
    bsz, t_len, _ = x.shape
    n = bsz * t_len
    assert t_len <= 8
    out, ns = pl.pallas_call(
        functools.partial(_conv_sample_kernel, bsz=bsz, t_len=t_len),
        out_shape=[jax.ShapeDtypeStruct((n, D_MODEL), F32), jax.ShapeDtypeStruct((bsz, HALO, D_MODEL), F32)],
        scratch_shapes=[pltpu.VMEM((bsz, HALO_PAD + 8, D_MODEL), F32), pltpu.VMEM((n, D_MODEL), F32),
                        pltpu.VMEM((n, D_MODEL), F32)],
        compiler_params=pltpu.CompilerParams(vmem_limit_bytes=VMEM_LIMIT),
        name="conv_layer_sample",
    )(x.reshape(n, D_MODEL), state, w_in, b_in, w_dw, b_dw, cg, cb, w_out, b_out, lg, lb)
    return out.reshape(x.shape), ns


def _ffn_kernel(x_ref, wg_ref, wu_ref, wd_ref, g_ref, b_ref, o_ref, acc_ref, xb_ref):
    f = pl.program_id(1)

    @pl.when(f == 0)
    def _():
        acc_ref[...] = jnp.zeros_like(acc_ref)
        xb_ref[...] = x_ref[...].astype(BF16)

    xb = xb_ref[...]
    h = _silu(_dot(xb, wg_ref[...])) * _dot(xb, wu_ref[...])
    acc_ref[...] += _dot(h.astype(BF16), wd_ref[...])

    @pl.when(f == pl.num_programs(1) - 1)
    def _():
        o_ref[...] = _ln(ALPHA * x_ref[...] + acc_ref[...], g_ref[...], b_ref[...])


def _ffn_layer(x, wg, wu, wd, g, b, *, tm=512, n_f=2):
    n = x.shape[0]
    d_ff = wg.shape[1]
    tm = min(tm, n)
    tf = d_ff // n_f
    assert n % tm == 0 and d_ff % n_f == 0 and tf % LANES == 0
    return pl.pallas_call(
        _ffn_kernel,
        grid=(n // tm, n_f),
        in_specs=[
            pl.BlockSpec((tm, D_MODEL), lambda i, f: (i, 0)),
            pl.BlockSpec((D_MODEL, tf), lambda i, f: (0, f)),
            pl.BlockSpec((D_MODEL, tf), lambda i, f: (0, f)),
            pl.BlockSpec((tf, D_MODEL), lambda i, f: (f, 0)),
            _full((1, D_MODEL)), _full((1, D_MODEL)),
        ],
        out_specs=pl.BlockSpec((tm, D_MODEL), lambda i, f: (i, 0)),
        out_shape=jax.ShapeDtypeStruct((n, D_MODEL), F32),
        scratch_shapes=[pltpu.VMEM((tm, D_MODEL), F32), pltpu.VMEM((tm, D_MODEL), BF16)],
        compiler_params=_params(("arbitrary", "arbitrary")),
        name="ffn_dense",
    )(x, wg, wu, wd, g, b)


def _rope_tables(pos):
    half = HEAD_DIM // 2
    inv = 1.0 / (ROPE_THETA ** (jnp.arange(half, dtype=F32) * (2.0 / HEAD_DIM)))
    ang = pos.astype(F32)[:, None] * inv[None, :]
    cos, sin = jnp.cos(ang), jnp.sin(ang)
    return jnp.tile(cos, (1, 4)), jnp.tile(jnp.concatenate([-sin, sin], axis=1), (1, 2))


def _rope(x, cos, sin, first_half):
    half = HEAD_DIM // 2
    out = []
    for c0 in range(0, x.shape[1], LANES):
        slab = x[:, c0:c0 + LANES]
        partner = jnp.where(first_half, pltpu.roll(slab, LANES - half, 1), pltpu.roll(slab, half, 1))
        out.append(slab * cos + partner * sin)
    return jnp.concatenate(out, axis=1)


def _store_dilated(dst_ref, val, s_ref, dil):
    if dil == 1:
        dst_ref[...] = val.astype(BF16)
        return
    d_att = N_HEADS * HEAD_DIM
    rows = val.shape[0] // dil
    for c in range(d_att // LANES):
        s_ref[c] = val[:, c * LANES:(c + 1) * LANES]
    for r in range(dil):
        for c in range(d_att // LANES):
            col = r * d_att + c * LANES
            dst_ref[:, col:col + LANES] = s_ref[c, pl.ds(r, rows, stride=dil), :].astype(BF16)


def _qkv_kernel(x_ref, cos_ref, sin_ref, wkv_ref, wq_ref, kv_ref, *rest, dilated):
    xb = x_ref[...].astype(BF16)
    cos, sin = cos_ref[...], sin_ref[...]
    first_half = (lax.broadcasted_iota(jnp.int32, cos.shape, 1) % HEAD_DIM) < HEAD_DIM // 2
    d_att = N_HEADS * HEAD_DIM
    kv = _dot(xb, wkv_ref[...])
    k = _rope(kv[:, :d_att], cos, sin, first_half)
    v = kv[:, d_att:]
    kv_ref[:, :d_att] = k
    kv_ref[:, d_att:] = v
    q = _rope(_dot(xb, wq_ref[...]), cos, sin, first_half) * (HEAD_DIM ** -0.5)
    if not dilated:
        rest[0][...] = q.astype(BF16)
        return
    s_ref = rest[-1]
    for g, (_, dil) in enumerate(GROUPS):
        q_ref, k_ref, v_ref = rest[3 * g:3 * g + 3]
        _store_dilated(q_ref, q[:, g * d_att:(g + 1) * d_att], s_ref, dil)
        _store_dilated(k_ref, k, s_ref, dil)
        _store_dilated(v_ref, v, s_ref, dil)


def _qkv_layer(x, pos, w_kv, w_q, *, dilated, tm=256):
    n = x.shape[0]
    tm = min(tm, n)
    assert n % tm == 0
    d_att = N_HEADS * HEAD_DIM
    cos, sin = _rope_tables(pos)
    out_specs = [pl.BlockSpec((tm, 2 * d_att), lambda i: (i, 0))]
    out_shape = [jax.ShapeDtypeStruct((n, 2 * d_att), F32)]
    if dilated:
        for _, dil in GROUPS:
            assert tm % (16 * dil) == 0
            out_specs += [pl.BlockSpec((tm // dil, dil * d_att), lambda i: (i, 0))] * 3
            out_shape += [jax.ShapeDtypeStruct((n // dil, dil * d_att), BF16)] * 3
    else:
        out_specs.append(pl.BlockSpec((tm, N_GROUPS * d_att), lambda i: (i, 0)))
        out_shape.append(jax.ShapeDtypeStruct((n, N_GROUPS * d_att), BF16))
    return pl.pallas_call(
        functools.partial(_qkv_kernel, dilated=dilated),
        grid=(n // tm,),
        in_specs=[
            pl.BlockSpec((tm, D_MODEL), lambda i: (i, 0)),
            pl.BlockSpec((tm, LANES), lambda i: (i, 0)),
            pl.BlockSpec((tm, LANES), lambda i: (i, 0)),
            _full((D_MODEL, 2 * d_att)), _full((D_MODEL, N_GROUPS * d_att)),
        ],
        out_specs=out_specs,
        out_shape=out_shape,
        scratch_shapes=[pltpu.VMEM((d_att // LANES, tm, LANES), F32)] if dilated else [],
        compiler_params=_params(("arbitrary",)),
        name="qkv_rope",
    )(x, cos, sin, w_kv, w_q)


ATTN_STEP_ROWS = 512
ATTN_SUB_ROWS = 128


def _attn_prompt_kernel(q_ref, kp_ref, kc_ref, vp_ref, vc_ref, o_ref, lse_ref, kx_ref, vx_ref, *, rows, sub):
    blk = pl.program_id(1)
    back = KEYS_PER_GROUP
    span = back + sub
    kx_ref[0:back, :] = kp_ref[...]
    kx_ref[back:back + rows, :] = kc_ref[...]
    vx_ref[0:back, :] = vp_ref[...]
    vx_ref[back:back + rows, :] = vc_ref[...]
    a = lax.broadcasted_iota(jnp.int32, (sub, span), 0)
    c = lax.broadcasted_iota(jnp.int32, (sub, span), 1)
    band = (c >= a) & (c <= a + back)
    lane = lax.broadcasted_iota(jnp.int32, (sub, LANES), 1)
    low = lane < HEAD_DIM

    for u in range(rows // sub):
        r0 = u * sub
        valid = band & ((blk > 0) | (c + r0 >= back)) if r0 < back else band
        lse_all = jnp.zeros((sub, LANES), F32)
        for p in range(N_HEADS // 2):
            sl = slice(p * LANES, (p + 1) * LANES)
            qp = q_ref[pl.ds(r0, sub), sl]
            kk = kx_ref[pl.ds(r0, span), sl]
            vv = vx_ref[pl.ds(r0, span), sl]
            outs = []
            for j in range(2):
                qm = jnp.where(low if j == 0 else jnp.logical_not(low), qp, jnp.zeros_like(qp))
                s = lax.dot_general(qm, kk, (((1,), (1,)), ((), ())), preferred_element_type=F32)
                s = jnp.where(valid, s, NEG)
                m = jnp.max(s, axis=1, keepdims=True)
                e = jnp.exp(s - m)
                den = jnp.sum(e, axis=1, keepdims=True)
                outs.append(_dot(e.astype(BF16), vv) / den)
                lse_all = jnp.where(lane == 2 * p + j, m + jnp.log(den), lse_all)
            o_ref[pl.ds(r0, sub), sl] = jnp.where(low, outs[0], outs[1]).astype(BF16)
        lse_ref[0, pl.ds(r0, sub), :] = lse_all


def _attn_prompt_group(qg, kg, vg, g):
    dil = GROUPS[g][1]
    back = KEYS_PER_GROUP
    assert GROUPS[g][0] // dil == back
    d_att = N_HEADS * HEAD_DIM
    n_rows = kg.shape[0]
    rows = min(ATTN_STEP_ROWS, n_rows)
    sub = ATTN_SUB_ROWS
    assert kg.shape[1] == dil * d_att and n_rows % rows == 0 and rows % back == 0 and rows % sub == 0
    cur = pl.BlockSpec((rows, d_att), lambda r, b: (b, r))
    prev = pl.BlockSpec((back, d_att), lambda r, b: (jnp.maximum(b * (rows // back) - 1, 0), r))
    return pl.pallas_call(
        functools.partial(_attn_prompt_kernel, rows=rows, sub=sub),
        grid=(dil, n_rows // rows),
        in_specs=[cur, prev, cur, prev, cur],
        out_specs=[cur, pl.BlockSpec((1, rows, LANES), lambda r, b: (r, b, 0))],
        out_shape=[jax.ShapeDtypeStruct((n_rows, dil * d_att), BF16),
                   jax.ShapeDtypeStruct((dil, n_rows, LANES), F32)],
        scratch_shapes=[pltpu.VMEM((back + rows, d_att), BF16), pltpu.VMEM((back + rows, d_att), BF16)],
        compiler_params=_params(("arbitrary", "arbitrary")),
        name=f"attn_prompt_g{g}",
    )(qg, kg, kg, vg, vg)


SAMPLE_ROWS = 16


def _sample_valid(t_len, past):
    key = np.arange(past + LANES)
    valid = np.zeros((SAMPLE_ROWS, past + LANES), np.float32)
    for r in range(N_GROUPS * t_len):
        g, i = divmod(r, t_len)
        window, dil = GROUPS[g]
        dist = past + i - key
        ok = (dist >= 0) & (dist <= window) & (dist % dil == 0) & (key < past + t_len)
        assert ok.sum() == window // dil + 1
        valid[r] = ok
    return valid[:, :past], valid[:, past:]


def _group_reduce(x, op):
    quarter = x.shape[0] // 4
    return op(op(x, pltpu.roll(x, quarter, 0)), op(pltpu.roll(x, 2 * quarter, 0), pltpu.roll(x, 3 * quarter, 0)))


def _attn_sample_kernel(q_ref, kvn_ref, vc_ref, vn_ref, cache_ref, o_ref, *, t_len):
    d_att = N_HEADS * HEAD_DIM
    q = q_ref[0].astype(F32)
    head = lax.broadcasted_iota(jnp.int32, (N_HEADS, d_att), 0)
    lane = lax.broadcasted_iota(jnp.int32, (N_HEADS, d_att), 1)
    own = head == lane // HEAD_DIM
    blocks = []
    for r in range(SAMPLE_ROWS):
        g, i = divmod(r, t_len)
        if g < N_GROUPS:
            qr = q[i:i + 1, g * d_att:(g + 1) * d_att]
            blocks.append(jnp.where(own, jnp.broadcast_to(qr, (N_HEADS, d_att)), 0.0))
        else:
            blocks.append(jnp.zeros((N_HEADS, d_att), F32))
    qbd = jnp.concatenate(blocks, axis=0).astype(BF16)
    kvn = jnp.concatenate([kvn_ref[0], jnp.zeros((LANES - t_len, 2 * d_att), F32)], axis=0).astype(BF16)
    s_c = jnp.where(vc_ref[...] > 0.0, _dot(qbd, cache_ref[0, 0].astype(BF16)), NEG)
    s_n = jnp.where(vn_ref[...] > 0.0, lax.dot_general(qbd, kvn[:, :d_att], (((1,), (1,)), ((), ())),
                                                       preferred_element_type=F32), NEG)
    ones = jnp.ones((s_n.shape[0], LANES), F32)
    m = jnp.maximum(jnp.max(s_c, axis=1, keepdims=True), jnp.max(s_n, axis=1, keepdims=True))
    m = _group_reduce(m * ones, jnp.maximum)[:, 0:1]
    e_c, e_n = jnp.exp(s_c - m), jnp.exp(s_n - m)
    den = jnp.sum(e_c, axis=1, keepdims=True) + jnp.sum(e_n, axis=1, keepdims=True)
    den = _group_reduce(den * ones, jnp.add)[:, 0:1]
    o = lax.dot_general((e_c / den).astype(BF16), cache_ref[0, 1].astype(BF16), (((1,), (1,)), ((), ())),
                        preferred_element_type=F32)
    o = o + _dot((e_n / den).astype(BF16), kvn[:, d_att:])
    o = jnp.sum(jnp.where(own[None], o.reshape(SAMPLE_ROWS, N_HEADS, d_att), 0.0), axis=1)
    out = o[0:t_len]
    for g in range(1, N_GROUPS):
        out = out + o[g * t_len:(g + 1) * t_len]
    o_ref[0] = out


def _attn_sample(q, kv_new, cache_t):
    bsz, t_len, _ = q.shape
    past = cache_t.shape[-1]
    d_att = N_HEADS * HEAD_DIM
    assert N_GROUPS * t_len <= SAMPLE_ROWS and SAMPLE_ROWS % 4 == 0 and SAMPLE_ROWS // 4 == t_len
    assert PAST_LEN >= past and past >= max(w for w, _ in GROUPS)
    valid_c, valid_n = (np.repeat(v, N_HEADS, axis=0) for v in _sample_valid(t_len, past))
    return pl.pallas_call(
        functools.partial(_attn_sample_kernel, t_len=t_len),
        grid=(bsz,),
        in_specs=[
            pl.BlockSpec((1, t_len, N_GROUPS * d_att), lambda b: (b, 0, 0)),
            pl.BlockSpec((1, t_len, 2 * d_att), lambda b: (b, 0, 0)),
            _full(valid_c.shape), _full(valid_n.shape),
            pl.BlockSpec((1, 2, d_att, past), lambda b: (b, 0, 0, 0)),
        ],
        out_specs=pl.BlockSpec((1, t_len, d_att), lambda b: (b, 0, 0)),
        out_shape=jax.ShapeDtypeStruct((bsz, t_len, d_att), F32),
        compiler_params=_params(("arbitrary",)),
        name="attn_sample",
    )(q, kv_new, jnp.asarray(valid_c), jnp.asarray(valid_n), cache_t)


def _store_token_tiles(ref, x):
    n = x.shape[0]
    for s in range(TILE_ROWS):
        ref[pl.ds(s, n, stride=TILE_ROWS), :] = x[:, s * LANES:(s + 1) * LANES]


def _load_token_slab(ref, n, s):
    return ref[pl.ds(s, n, stride=TILE_ROWS), :]


def _token_rows(ref, t):
    return ref.at[pl.ds(pl.multiple_of(t * TILE_ROWS, TILE_ROWS), TILE_ROWS), :]


def _split3(w):
    hi = w.astype(BF16)
    r1 = w - hi.astype(F32)
    mid = r1.astype(BF16)
    lo = (r1 - mid.astype(F32)).astype(BF16)
    return hi, mid, lo


def _oproj_router_tail(att, x_ref, wo_ref, g_ref, b_ref, wrh_ref, wrl_ref, x3t_ref, route_ref, cnt_ref):
    tm = att.shape[0]

    @pl.when(pl.program_id(0) == 0)
    def _():
        cnt_ref[...] = jnp.zeros_like(cnt_ref)

    x3 = _ln(ALPHA * x_ref[...] + _dot(att.astype(BF16), wo_ref[...]), g_ref[...], b_ref[...])
    _store_token_tiles(x3t_ref, x3)
    xh = x3.astype(BF16)
    xl = (x3 - xh.astype(F32)).astype(BF16)
    logits = _dot(xh, wrh_ref[...]) + (_dot(xh, wrl_ref[...]) + _dot(xl, wrh_ref[...]))
    lane = lax.broadcasted_iota(jnp.int32, logits.shape, 1)
    logits = jnp.where(lane < N_EXPERTS, logits, NEG)
    v1 = jnp.max(logits, axis=1, keepdims=True)
    i1 = jnp.min(jnp.where(logits == v1, lane, LANES), axis=1, keepdims=True)
    rest = jnp.where(lane == i1, NEG, logits)
    v2 = jnp.max(rest, axis=1, keepdims=True)
    i2 = jnp.min(jnp.where(rest == v2, lane, LANES), axis=1, keepdims=True)
    e2 = jnp.exp(v2 - v1)
    den = 1.0 + e2
    sel = jnp.where((lane == i1) | (lane == i2), 1.0, 0.0)
    tri = jnp.where(lax.broadcasted_iota(jnp.int32, (tm, tm), 1) < lax.broadcasted_iota(jnp.int32, (tm, tm), 0),
                    1.0, 0.0).astype(BF16)
    before = _dot(tri, sel.astype(BF16)) + cnt_ref[0:1, :]
    rank1 = jnp.sum(jnp.where(lane == i1, before, 0.0), axis=1, keepdims=True)
    rank2 = jnp.sum(jnp.where(lane == i2, before, 0.0), axis=1, keepdims=True)
    cnt_ref[0:1, :] = cnt_ref[0:1, :] + jnp.sum(sel, axis=0, keepdims=True)
    fields = (i1.astype(F32), i2.astype(F32), 1.0 / den, e2 / den, rank1, rank2)
    route = jnp.zeros(logits.shape, F32)
    for k, val in enumerate(fields):
        route = jnp.where(lane == k, val, route)
    route_ref[...] = route


def _load_dilated(o_ref, lse_ref, o_scr, l_scr, dil):
    if dil == 1:
        return o_ref[...].astype(F32), lse_ref[0]
    d_att = N_HEADS * HEAD_DIM
    rows = o_ref.shape[0]
    n_slabs = d_att // LANES
    for r in range(dil):
        for c in range(n_slabs):
            col = r * d_att + c * LANES
            o_scr[c, pl.ds(r, rows, stride=dil), :] = o_ref[:, col:col + LANES].astype(F32)
        l_scr[pl.ds(r, rows, stride=dil), :] = lse_ref[r]
    return jnp.concatenate([o_scr[c] for c in range(n_slabs)], axis=1), l_scr[...]


def _oproj_prompt_kernel(o0_ref, l0_ref, o1_ref, l1_ref, o2_ref, l2_ref, ex_ref, x_ref, wo_ref, g_ref, b_ref,
                         wrh_ref, wrl_ref, x3t_ref, route_ref, cnt_ref, *scr):
    parts = [_load_dilated(o_ref, l_ref, scr[2 * g], scr[2 * g + 1], GROUPS[g][1])
             for g, (o_ref, l_ref) in enumerate(((o0_ref, l0_ref), (o1_ref, l1_ref), (o2_ref, l2_ref)))]
    lse = [l for _, l in parts]
    m = jnp.maximum(jnp.maximum(lse[0], lse[1]), lse[2])
    w = [jnp.exp(l - m) for l in lse]
    den = w[0] + w[1] + w[2]
    att = None
    for g, (o, _) in enumerate(parts):
        hi, mid, lo = _split3(w[g] / den)
        wexp = _dot(hi, ex_ref[...]) + (_dot(mid, ex_ref[...]) + _dot(lo, ex_ref[...]))
        term = wexp * o
        att = term if att is None else att + term
    _oproj_router_tail(att, x_ref, wo_ref, g_ref, b_ref, wrh_ref, wrl_ref, x3t_ref, route_ref, cnt_ref)


def _oproj_sample_kernel(att_ref, x_ref, wo_ref, g_ref, b_ref, wrh_ref, wrl_ref, x3t_ref, route_ref, cnt_ref):
    _oproj_router_tail(att_ref[...], x_ref, wo_ref, g_ref, b_ref, wrh_ref, wrl_ref, x3t_ref, route_ref, cnt_ref)


def _router_split(w_router):
    wr = jnp.zeros((D_MODEL, LANES), F32).at[:, :N_EXPERTS].set(w_router)
    hi = wr.astype(BF16)
    return hi, (wr - hi.astype(F32)).astype(BF16)


def _oproj_router(att_parts, x, w_o, g, b, w_router, *, tm=256):
    n = x.shape[0]
    tm = min(tm, n)
    assert n % tm == 0
    d_att = N_HEADS * HEAD_DIM
    wrh, wrl = _router_split(w_router)
    row_blk = lambda w: pl.BlockSpec((tm, w), lambda i: (i, 0))
    tail_specs = [row_blk(D_MODEL), _full((d_att, D_MODEL)), _full((1, D_MODEL)), _full((1, D_MODEL)),
                  _full((D_MODEL, LANES)), _full((D_MODEL, LANES))]
    tail_args = (x, w_o, g, b, wrh, wrl)
    scratch = []
    if len(att_parts) == 1:
        kern, specs, args = _oproj_sample_kernel, [row_blk(d_att)], tuple(att_parts)
    else:
        expand = (np.arange(LANES)[:, None] == np.arange(d_att)[None, :] // HEAD_DIM).astype(np.float32)
        kern, specs, args = _oproj_prompt_kernel, [], []
        for (_, dil), (o, lse) in zip(GROUPS, att_parts):
            specs += [pl.BlockSpec((tm // dil, dil * d_att), lambda i: (i, 0)),
                      pl.BlockSpec((dil, tm // dil, LANES), lambda i: (0, i, 0))]
            args += [o, lse]
            scratch += [pltpu.VMEM((d_att // LANES, tm, LANES), F32), pltpu.VMEM((tm, LANES), F32)]
        specs.append(_full((LANES, d_att)))
        args.append(jnp.asarray(expand, BF16))
    return pl.pallas_call(
        kern,
        grid=(n // tm,),
        in_specs=specs + tail_specs,
        out_specs=[pl.BlockSpec((tm * TILE_ROWS, LANES), lambda i: (i, 0)), row_blk(LANES), _full((TILE_ROWS, LANES))],
        out_shape=[jax.ShapeDtypeStruct((n * TILE_ROWS, LANES), F32), jax.ShapeDtypeStruct((n, LANES), F32),
                   jax.ShapeDtypeStruct((TILE_ROWS, LANES), F32)],
        scratch_shapes=scratch,
        compiler_params=_params(("arbitrary",)),
        name="oproj_router",
    )(*args, *tail_args)


def _route_plan(routes, counts, tile, n_tiles):
    cnt = [c[0, :N_EXPERTS].astype(jnp.int32) for c in counts]
    total = sum(cnt)
    padded = (total + tile - 1) // tile * tile
    ends = jnp.cumsum(padded)
    offs = ends - padded
    fill = jnp.concatenate([offs + total, ends]).astype(jnp.int32)
    pos = []
    for route, c in zip(routes, cnt):
        e = route[:, 0:2].astype(jnp.int32)
        pos.append((offs[e] + route[:, 4:6].astype(jnp.int32)).reshape(-1))
        offs = offs + c
    n_live = (ends[-1] // tile).astype(jnp.int32)
    tile_id = jnp.minimum(jnp.arange(n_tiles, dtype=jnp.int32), n_live - 1)
    tile_expert = jnp.sum((ends // tile)[None, :] <= tile_id[:, None], axis=1).astype(jnp.int32)
    return pos, tile_expert, n_live.reshape(1), fill


def _dispatch_kernel(pos_ref, fill_ref, xa_ref, xb_ref, xs_ref, z_ref, sem, *, tm, steps_a, n_b, tile, n_tiles):
    i = pl.program_id(0)

    def scatter(src_ref, count, first_token):
        def row_copy(r, s):
            dst = _token_rows(xs_ref, pos_ref[(first_token + r) * 2 + s])
            return pltpu.make_async_copy(_token_rows(src_ref, r), dst, sem)

        def issue(r, c):
            row_copy(r, 0).start(priority=0)
            row_copy(r, 1).start(priority=1)
            return c

        def drain(r, c):
            row_copy(r, 0).wait()
            row_copy(r, 1).wait()
            return c

        lax.fori_loop(0, count, issue, 0)
        lax.fori_loop(0, count, drain, 0)

    @pl.when(i < steps_a)
    def _():
        scatter(xa_ref, tm, i * tm)

    @pl.when(i == steps_a)
    def _():
        scatter(xb_ref, n_b, steps_a * tm)
        z_ref[...] = jnp.zeros_like(z_ref)
        zero_row = lambda r: pltpu.make_async_copy(z_ref.at[0:TILE_ROWS, :], _token_rows(xs_ref, r), sem)
        tile_rows = tile * TILE_ROWS
        zero_tile = lambda j: pltpu.make_async_copy(
            z_ref, xs_ref.at[pl.ds(pl.multiple_of(j * tile_rows, tile_rows), tile_rows), :], sem)

        def loop(lo, hi, copy):
            def issue(r, c):
                copy(r).start()
                return c

            def drain(r, c):
                copy(r).wait()
                return c

            lax.fori_loop(lo, hi, issue, 0)
            lax.fori_loop(lo, hi, drain, 0)

        for e in range(N_EXPERTS):
            loop(fill_ref[e], fill_ref[N_EXPERTS + e], zero_row)
        loop(fill_ref[2 * N_EXPERTS - 1] // tile, n_tiles, zero_tile)


def _dispatch(pos, fill_rng, x3t_a, x3t_b, n_tiles, tile, *, tm=256):
    n_a, n_b = x3t_a.shape[0] // TILE_ROWS, x3t_b.shape[0] // TILE_ROWS
    tm = min(tm, n_a)
    assert n_a % tm == 0
    steps_a = n_a // tm
    return pl.pallas_call(
        functools.partial(_dispatch_kernel, tm=tm, steps_a=steps_a, n_b=n_b, tile=tile, n_tiles=n_tiles),
        grid_spec=pltpu.PrefetchScalarGridSpec(
            num_scalar_prefetch=2, grid=(steps_a + 1,),
            in_specs=[pl.BlockSpec((tm * TILE_ROWS, LANES), lambda i, *_: (jnp.minimum(i, steps_a - 1), 0)),
                      pl.BlockSpec((n_b * TILE_ROWS, LANES), lambda i, *_: (0, 0))],
            out_specs=pl.BlockSpec(memory_space=pl.ANY),
            scratch_shapes=[pltpu.VMEM((tile * TILE_ROWS, LANES), F32), pltpu.SemaphoreType.DMA(())]),
        out_shape=jax.ShapeDtypeStruct((n_tiles * tile * TILE_ROWS, LANES), F32),
        compiler_params=_params(("arbitrary",)),
        name="moe_dispatch",
    )(pos, fill_rng, x3t_a, x3t_b)


def _moe_ffn_kernel(te_ref, nl_ref, xs_ref, wg_ref, wu_ref, wd_ref, ys_ref, acc_ref, xb_ref, *, tile):
    j, f = pl.program_id(0), pl.program_id(1)

    @pl.when(j < nl_ref[0])
    def _():
        @pl.when(f == 0)
        def _():
            for s in range(TILE_ROWS):
                xb_ref[:, s * LANES:(s + 1) * LANES] = _load_token_slab(xs_ref, tile, s).astype(BF16)

        xb = xb_ref[...]
        h = _silu(_dot(xb, wg_ref[...])) * _dot(xb, wu_ref[...])
        y = _dot(h.astype(BF16), wd_ref[...])

        @pl.when(f == 0)
        def _():
            acc_ref[...] = y

        @pl.when(f > 0)
        def _():
            acc_ref[...] += y

        @pl.when(f == pl.num_programs(1) - 1)
        def _():
            _store_token_tiles(ys_ref, acc_ref[...])

    @pl.when((j >= nl_ref[0]) & (f == pl.num_programs(1) - 1))
    def _():
        ys_ref[...] = jnp.zeros_like(ys_ref)


def _moe_ffn(xs, tile_expert, n_live, wg, wu, wd, *, tile, n_f=2):
    n_tiles = tile_expert.shape[0]
    d_ff = wg.shape[2]
    tf = d_ff // n_f
    assert d_ff % n_f == 0 and tf % LANES == 0 and xs.shape[0] == n_tiles * tile * TILE_ROWS
    row_map = lambda j, f, te, nl: (jnp.minimum(j, nl[0] - 1), 0)
    f_idx = lambda j, f, nl: jnp.where(j < nl[0], f, n_f - 1)
    return pl.pallas_call(
        functools.partial(_moe_ffn_kernel, tile=tile),
        grid_spec=pltpu.PrefetchScalarGridSpec(
            num_scalar_prefetch=2, grid=(n_tiles, n_f),
            in_specs=[
                pl.BlockSpec((tile * TILE_ROWS, LANES), row_map),
                pl.BlockSpec((None, D_MODEL, tf), lambda j, f, te, nl: (te[j], 0, f_idx(j, f, nl))),
                pl.BlockSpec((None, D_MODEL, tf), lambda j, f, te, nl: (te[j], 0, f_idx(j, f, nl))),
                pl.BlockSpec((None, tf, D_MODEL), lambda j, f, te, nl: (te[j], f_idx(j, f, nl), 0)),
            ],
            out_specs=pl.BlockSpec((tile * TILE_ROWS, LANES), lambda j, f, te, nl: (j, 0)),
            scratch_shapes=[pltpu.VMEM((tile, D_MODEL), F32), pltpu.VMEM((tile, D_MODEL), BF16)]),
        out_shape=jax.ShapeDtypeStruct(xs.shape, F32),
        compiler_params=_params(("arbitrary", "arbitrary")),
        name="moe_ffn",
    )(tile_expert, n_live, xs, wg, wu, wd)


def _moe_combine_kernel(pos_ref, route_ref, x3t_ref, ys_ref, g_ref, b_ref, o_ref, y1_ref, y2_ref, pre_ref, sem, *, tm):
    i = pl.program_id(0)
    bufs = (y1_ref, y2_ref)

    def row_copy(r, s):
        return pltpu.make_async_copy(_token_rows(ys_ref, pos_ref[(i * tm + r) * 2 + s]), _token_rows(bufs[s], r), sem)

    def issue(r, c):
        row_copy(r, 0).start(priority=0)
        row_copy(r, 1).start(priority=1)
        return c

    def drain(r, c):
        row_copy(r, 0).wait()
        row_copy(r, 1).wait()
        return c

    lax.fori_loop(0, tm, issue, 0)
    lax.fori_loop(0, tm, drain, 0)
    route = route_ref[...]
    g1, g2 = route[:, 2:3], route[:, 3:4]
    for s in range(TILE_ROWS):
        y = g1 * _load_token_slab(y1_ref, tm, s) + g2 * _load_token_slab(y2_ref, tm, s)
        pre_ref[:, s * LANES:(s + 1) * LANES] = ALPHA * _load_token_slab(x3t_ref, tm, s) + y
    o_ref[...] = _ln(pre_ref[...], g_ref[...], b_ref[...])


def _moe_combine(pos, route, x3t, ys, g, b, *, tm=256):
    n = route.shape[0]
    tm = min(tm, n)
    assert n % tm == 0
    tok_buf = pltpu.VMEM((tm * TILE_ROWS, LANES), F32)
    return pl.pallas_call(
        functools.partial(_moe_combine_kernel, tm=tm),
        grid_spec=pltpu.PrefetchScalarGridSpec(
            num_scalar_prefetch=1, grid=(n // tm,),
            in_specs=[
                pl.BlockSpec((tm, LANES), lambda i, *_: (i, 0)),
                pl.BlockSpec((tm * TILE_ROWS, LANES), lambda i, *_: (i, 0)),
                pl.BlockSpec(memory_space=pl.ANY),
                pl.BlockSpec((1, D_MODEL), lambda i, *_: (0, 0)),
                pl.BlockSpec((1, D_MODEL), lambda i, *_: (0, 0)),
            ],
            out_specs=pl.BlockSpec((tm, D_MODEL), lambda i, *_: (i, 0)),
            scratch_shapes=[tok_buf, tok_buf, pltpu.VMEM((tm, D_MODEL), F32), pltpu.SemaphoreType.DMA(())]),
        out_shape=jax.ShapeDtypeStruct((n, D_MODEL), F32),
        compiler_params=_params(("arbitrary",)),
        name="moe_combine",
    )(pos, route, x3t, ys, g, b)


def kernel(x_prompt, x_sample, cache_kv, state_conv, conv_w_in, conv_b_in, conv_w_dw, conv_b_dw, conv_ln_g,
           conv_ln_b, conv_w_out, conv_b_out, ffn_w_gate, ffn_w_up, ffn_w_down, w_kv, attn_w_q, attn_w_o,
           moe_w_router, moe_w_gate, moe_w_up, moe_w_down, ln_g, ln_b):
    bp, t_p, _ = x_prompt.shape
    bs, t_s, _ = x_sample.shape
    assert bp == 1 and conv_w_in.shape[0] == 1 and attn_w_q.shape[0] == 1
    d_att = N_HEADS * HEAD_DIM
    row = lambda a: a.reshape(1, -1)
    bf = lambda a: a.astype(BF16)

    conv_w = (bf(conv_w_in[0]), row(conv_b_in[0]), conv_w_dw[0], row(conv_b_dw[0]), row(conv_ln_g[0]),
              row(conv_ln_b[0]), bf(conv_w_out[0]), row(conv_b_out[0]), row(ln_g[0, 0]), row(ln_b[0, 0]))
    ffn_w = (bf(ffn_w_gate[0]), bf(ffn_w_up[0]), bf(ffn_w_down[0]), row(ln_g[0, 1]), row(ln_b[0, 1]))
    wkv_b, wq_b, wo_b = bf(w_kv), bf(attn_w_q[0]), bf(attn_w_o[0])
    ln10 = (row(ln_g[1, 0]), row(ln_b[1, 0]))
    ln11 = (row(ln_g[1, 1]), row(ln_b[1, 1]))

    x1, conv_p = _conv_layer_prompt(x_prompt, jnp.zeros((bp, HALO, D_MODEL), F32), *conv_w)
    x2 = _ffn_layer(x1.reshape(t_p, D_MODEL), *ffn_w)
    kv_p, *qkv_views = _qkv_layer(x2, jnp.arange(t_p, dtype=jnp.int32), wkv_b, wq_b, dilated=True)
    parts = [_attn_prompt_group(*qkv_views[3 * g:3 * g + 3], g) for g in range(N_GROUPS)]
    x3t_p, route_p, cnt_p = _oproj_router(parts, x2, wo_b, *ln10, moe_w_router[0])
    n_keep = min(max(w for w, _ in GROUPS), t_p)
    kv_prompt = kv_p[t_p - n_keep:].reshape(bp, n_keep, 2, N_HEADS, HEAD_DIM)

    n_s = bs * t_s
    x1s, conv_s = _conv_layer_sample(x_sample, state_conv[0], *conv_w)
    x2s = _ffn_layer(x1s.reshape(n_s, D_MODEL), *ffn_w)
    pos_s = PAST_LEN + jnp.tile(jnp.arange(t_s, dtype=jnp.int32), bs)
    kv_s, q_s = _qkv_layer(x2s, pos_s, wkv_b, wq_b, dilated=False)
    att_s = _attn_sample(q_s.reshape(bs, t_s, N_GROUPS * d_att), kv_s.reshape(bs, t_s, 2 * d_att),
                         jnp.transpose(cache_kv, (0, 2, 3, 4, 1)).reshape(bs, 2, d_att, cache_kv.shape[1]))
    x3t_s, route_s, cnt_s = _oproj_router((att_s.reshape(n_s, d_att),), x2s, wo_b, *ln10, moe_w_router[0])
    kv_sample = kv_s.reshape(bs, t_s, 2, N_HEADS, HEAD_DIM)

    n_slots = 2 * (t_p + n_s)
    n_tiles = (n_slots + N_EXPERTS * (MOE_TILE - 1)) // MOE_TILE
    (pos_p, pos_sm), tile_expert, n_live, fill_rng = _route_plan((route_p, route_s), (cnt_p, cnt_s), MOE_TILE, n_tiles)
    xs = _dispatch(jnp.concatenate([pos_p, pos_sm]), fill_rng, x3t_p, x3t_s, n_tiles, MOE_TILE)
    ys = _moe_ffn(xs, tile_expert, n_live, bf(moe_w_gate[0]), bf(moe_w_up[0]), bf(moe_w_down[0]), tile=MOE_TILE)
    y_prompt = _moe_combine(pos_p, route_p, x3t_p, ys, *ln11).reshape(bp, t_p, D_MODEL)
    y_sample = _moe_combine(pos_sm, route_s, x3t_s, ys, *ln11).reshape(bs, t_s, D_MODEL)

    return (y_prompt, y_sample, conv_p[None], conv_s[None], kv_prompt, kv_sample)
```

```python
import functools

import numpy as np
import jax
import jax.numpy as jnp
from jax import lax
from jax.experimental import pallas as pl
from jax.experimental.pallas import tpu as pltpu

F32 = jnp.float32
BF16 = jnp.bfloat16

D_MODEL = 1024
N_HEADS = 16
HEAD_DIM = 64
GROUPS = ((128, 1), (512, 4), (2048, 16))
N_GROUPS = len(GROUPS)
KEYS_PER_GROUP = 128
CONV_WIDTH = 31
HALO = CONV_WIDTH - 1
N_EXPERTS = 8
ROPE_THETA = 10000.0
LN_EPS = 1e-5
DEPTH = 2
ALPHA = (2 * DEPTH) ** 0.25
PAST_LEN = 16384
NEG = -1e30

LANES = 128
HALO_PAD = 32
TILE_ROWS = 8
MOE_TILE = 512
VMEM_LIMIT = 56 * 1024 * 1024


def _params(sem, vmem=VMEM_LIMIT):
    return pltpu.CompilerParams(dimension_semantics=sem, vmem_limit_bytes=vmem)


def _dot(a, b):
    return jnp.dot(a, b, preferred_element_type=F32)


def _sigmoid(x):
    return 1.0 / (1.0 + jnp.exp(-x))


def _silu(x):
    return x * _sigmoid(x)


def _ln(x, g, b):
    xc = x - jnp.mean(x, axis=-1, keepdims=True)
    var = jnp.mean(xc * xc, axis=-1, keepdims=True)
    return xc * lax.rsqrt(var + LN_EPS) * g + b


def _full(shape):
    return pl.BlockSpec(shape, lambda *_: (0,) * len(shape))


def _conv_taps(s_ref, wdw_ref, bdw_ref, c_ref, sh_ref, n_rows, row_chunk):
    base = HALO_PAD - HALO
    phases = [[k for k in range(CONV_WIDTH) if (base + k) % TILE_ROWS == ph] for ph in range(TILE_ROWS)]

    def chunk(rc, carry):
        r0 = pl.multiple_of(rc * row_chunk, row_chunk)
        win = s_ref.at[pl.ds(r0, row_chunk + HALO_PAD), :]
        for c0 in range(0, D_MODEL, LANES):
            for ph, taps in enumerate(phases):
                n_ext = row_chunk + taps[-1] - taps[0]
                sh_ref[ph, 0:n_ext, :] = win[base + taps[0]:base + taps[0] + n_ext, c0:c0 + LANES]
            acc = jnp.broadcast_to(bdw_ref[:, c0:c0 + LANES], (row_chunk, LANES))
            for ph, taps in enumerate(phases):
                for k in taps:
                    acc = acc + wdw_ref[k:k + 1, c0:c0 + LANES] * sh_ref[ph, k - taps[0]:k - taps[0] + row_chunk, :]
            c_ref[pl.ds(r0, row_chunk), c0:c0 + LANES] = acc
        return carry

    lax.fori_loop(0, n_rows // row_chunk, chunk, 0)


def _conv_prompt_kernel(x_ref, st_ref, win_ref, bin_ref, wdw_ref, bdw_ref, cg_ref, cb_ref, wout_ref, bout_ref,
                        lg_ref, lb_ref, o_ref, ns_ref, s_ref, c_ref, sh_ref, *, tm, row_chunk):
    base = HALO_PAD - HALO

    @pl.when(pl.program_id(1) == 0)
    def _():
        s_ref[0:HALO_PAD, :] = jnp.zeros((HALO_PAD, D_MODEL), F32)
        s_ref[base:HALO_PAD, :] = st_ref[0]

    x = x_ref[0]
    h = _dot(x.astype(BF16), win_ref[...]) + bin_ref[...]
    s_ref[HALO_PAD:HALO_PAD + tm, :] = h[:, :D_MODEL] * _sigmoid(h[:, D_MODEL:])
    _conv_taps(s_ref, wdw_ref, bdw_ref, c_ref, sh_ref, tm, row_chunk)
    c = _silu(_ln(c_ref[...], cg_ref[...], cb_ref[...]))
    mix = _dot(c.astype(BF16), wout_ref[...]) + bout_ref[...]
    o_ref[0] = _ln(ALPHA * x + mix, lg_ref[...], lb_ref[...])
    tail = s_ref[pl.ds(tm + base, HALO), :]
    s_ref[base:HALO_PAD, :] = tail
    ns_ref[0] = tail


def _conv_layer_prompt(x, state, w_in, b_in, w_dw, b_dw, cg, cb, w_out, b_out, lg, lb, *, tm=256):
    bsz, t_len, _ = x.shape
    tm = min(tm, t_len)
    assert t_len % tm == 0 and tm >= HALO_PAD and tm % 8 == 0
    row_chunk = 64 if tm % 64 == 0 else 8
    row = lambda n: _full((1, n))
    return pl.pallas_call(
        functools.partial(_conv_prompt_kernel, tm=tm, row_chunk=row_chunk),
        grid=(bsz, t_len // tm),
        in_specs=[
            pl.BlockSpec((1, tm, D_MODEL), lambda b, i: (b, i, 0)),
            pl.BlockSpec((1, HALO, D_MODEL), lambda b, i: (b, 0, 0)),
            _full((D_MODEL, 2 * D_MODEL)), row(2 * D_MODEL),
            _full((CONV_WIDTH, D_MODEL)), row(D_MODEL), row(D_MODEL), row(D_MODEL),
            _full((D_MODEL, D_MODEL)), row(D_MODEL), row(D_MODEL), row(D_MODEL),
        ],
        out_specs=[
            pl.BlockSpec((1, tm, D_MODEL), lambda b, i: (b, i, 0)),
            pl.BlockSpec((1, HALO, D_MODEL), lambda b, i: (b, 0, 0)),
        ],
        out_shape=[jax.ShapeDtypeStruct(x.shape, F32), jax.ShapeDtypeStruct((bsz, HALO, D_MODEL), F32)],
        scratch_shapes=[pltpu.VMEM((HALO_PAD + tm, D_MODEL), F32), pltpu.VMEM((tm, D_MODEL), F32),
                        pltpu.VMEM((TILE_ROWS, row_chunk + HALO_PAD, LANES), F32)],
        compiler_params=_params(("arbitrary", "arbitrary")),
        name="conv_layer_prompt",
    )(x, state, w_in, b_in, w_dw, b_dw, cg, cb, w_out, b_out, lg, lb)


def _conv_sample_kernel(x_ref, st_ref, win_ref, bin_ref, wdw_ref, bdw_ref, cg_ref, cb_ref, wout_ref, bout_ref,
                        lg_ref, lb_ref, o_ref, ns_ref, s_ref, u_ref, c_ref, *, bsz, t_len):
    base = HALO_PAD - HALO
    x = x_ref[...]
    h = _dot(x.astype(BF16), win_ref[...]) + bin_ref[...]
    u_ref[...] = h[:, :D_MODEL] * _sigmoid(h[:, D_MODEL:])
    s_ref[:, 0:HALO_PAD, :] = jnp.zeros((bsz, HALO_PAD, D_MODEL), F32)
    s_ref[:, base:HALO_PAD, :] = st_ref[...]
    for b in range(bsz):
        s_ref[b, HALO_PAD:HALO_PAD + t_len, :] = u_ref[b * t_len:(b + 1) * t_len, :]
    acc = jnp.broadcast_to(bdw_ref[...][None], (bsz, t_len, D_MODEL))
    for k in range(CONV_WIDTH):
        acc = acc + wdw_ref[k:k + 1, :][None] * s_ref[:, base + k:base + k + t_len, :]
    for b in range(bsz):
        c_ref[b * t_len:(b + 1) * t_len, :] = acc[b]
    c = _silu(_ln(c_ref[...], cg_ref[...], cb_ref[...]))
    mix = _dot(c.astype(BF16), wout_ref[...]) + bout_ref[...]
    o_ref[...] = _ln(ALPHA * x + mix, lg_ref[...], lb_ref[...])
    ns_ref[...] = s_ref[:, base + t_len:HALO_PAD + t_len, :]


def _conv_layer_sample(x, state, w_in, b_in, w_dw, b_dw, cg, cb, w_out, b_out, lg, lb):
    bsz, t_len, _ = x.shape
    n = bsz * t_len
    assert t_len <= 8
    out, ns = pl.pallas_call(
        functools.partial(_conv_sample_kernel, bsz=bsz, t_len=t_len),
        out_shape=[jax.ShapeDtypeStruct((n, D_MODEL), F32), jax.ShapeDtypeStruct((bsz, HALO, D_MODEL), F32)],
        scratch_shapes=[pltpu.VMEM((bsz, HALO_PAD + 8, D_MODEL), F32), pltpu.VMEM((n, D_MODEL), F32),
                        pltpu.VMEM((n, D_MODEL), F32)],
        compiler_params=pltpu.CompilerParams(vmem_limit_bytes=VMEM_LIMIT),
        name="conv_layer_sample",
    )(x.reshape(n, D_MODEL), state, w_in, b_in, w_dw, b_dw, cg, cb, w_out, b_out, lg, lb)
    return out.reshape(x.shape), ns


def _ffn_kernel(x_ref, wg_ref, wu_ref, wd_ref, g_ref, b_ref, o_ref, acc_ref, xb_ref):
    f = pl.program_id(1)

    @pl.when(f == 0)
    def _():
        acc_ref[...] = jnp.zeros_like(acc_ref)
        xb_ref[...] = x_ref[...].astype(BF16)

    xb = xb_ref[...]
    h = _silu(_dot(xb, wg_ref[...])) * _dot(xb, wu_ref[...])
    acc_ref[...] += _dot(h.astype(BF16), wd_ref[...])

    @pl.when(f == pl.num_programs(1) - 1)
    def _():
        o_ref[...] = _ln(ALPHA * x_ref[...] + acc_ref[...], g_ref[...], b_ref[...])


def _ffn_layer(x, wg, wu, wd, g, b, *, tm=512, n_f=2):
    n = x.shape[0]
    d_ff = wg.shape[1]
    tm = min(tm, n)
    tf = d_ff // n_f
    assert n % tm == 0 and d_ff % n_f == 0 and tf % LANES == 0
    return pl.pallas_call(
        _ffn_kernel,
        grid=(n // tm, n_f),
        in_specs=[
            pl.BlockSpec((tm, D_MODEL), lambda i, f: (i, 0)),
            pl.BlockSpec((D_MODEL, tf), lambda i, f: (0, f)),
            pl.BlockSpec((D_MODEL, tf), lambda i, f: (0, f)),
            pl.BlockSpec((tf, D_MODEL), lambda i, f: (f, 0)),
            _full((1, D_MODEL)), _full((1, D_MODEL)),
        ],
        out_specs=pl.BlockSpec((tm, D_MODEL), lambda i, f: (i, 0)),
        out_shape=jax.ShapeDtypeStruct((n, D_MODEL), F32),
        scratch_shapes=[pltpu.VMEM((tm, D_MODEL), F32), pltpu.VMEM((tm, D_MODEL), BF16)],
        compiler_params=_params(("arbitrary", "arbitrary")),
        name="ffn_dense",
    )(x, wg, wu, wd, g, b)


def _rope_tables(pos):
    half = HEAD_DIM // 2
    inv = 1.0 / (ROPE_THETA ** (jnp.arange(half, dtype=F32) * (2.0 / HEAD_DIM)))
    ang = pos.astype(F32)[:, None] * inv[None, :]
    cos, sin = jnp.cos(ang), jnp.sin(ang)
    return jnp.tile(cos, (1, 4)), jnp.tile(jnp.concatenate([-sin, sin], axis=1), (1, 2))


def _rope(x, cos, sin, first_half):
    half = HEAD_DIM // 2
    out = []
    for c0 in range(0, x.shape[1], LANES):
        slab = x[:, c0:c0 + LANES]
        partner = jnp.where(first_half, pltpu.roll(slab, LANES - half, 1), pltpu.roll(slab, half, 1))
        out.append(slab * cos + partner * sin)
    return jnp.concatenate(out, axis=1)


def _store_dilated(dst_ref, val, s_ref, dil):
    if dil == 1:
        dst_ref[...] = val.astype(BF16)
        return
    d_att = N_HEADS * HEAD_DIM
    rows = val.shape[0] // dil
    for c in range(d_att // LANES):
        s_ref[c] = val[:, c * LANES:(c + 1) * LANES]
    for r in range(dil):
        for c in range(d_att // LANES):
            col = r * d_att + c * LANES
            dst_ref[:, col:col + LANES] = s_ref[c, pl.ds(r, rows, stride=dil), :].astype(BF16)


def _qkv_kernel(x_ref, cos_ref, sin_ref, wkv_ref, wq_ref, kv_ref, *rest, dilated):
    xb = x_ref[...].astype(BF16)
    cos, sin = cos_ref[...], sin_ref[...]
    first_half = (lax.broadcasted_iota(jnp.int32, cos.shape, 1) % HEAD_DIM) < HEAD_DIM // 2
    d_att = N_HEADS * HEAD_DIM
    kv = _dot(xb, wkv_ref[...])
    k = _rope(kv[:, :d_att], cos, sin, first_half)
    v = kv[:, d_att:]
    kv_ref[:, :d_att] = k
    kv_ref[:, d_att:] = v
    q = _rope(_dot(xb, wq_ref[...]), cos, sin, first_half) * (HEAD_DIM ** -0.5)
    if not dilated:
        rest[0][...] = q.astype(BF16)
        return
    s_ref = rest[-1]
    for g, (_, dil) in enumerate(GROUPS):
        q_ref, k_ref, v_ref = rest[3 * g:3 * g + 3]
        _store_dilated(q_ref, q[:, g * d_att:(g + 1) * d_att], s_ref, dil)
        _store_dilated(k_ref, k, s_ref, dil)
        _store_dilated(v_ref, v, s_ref, dil)


def _qkv_layer(x, pos, w_kv, w_q, *, dilated, tm=256):
    n = x.shape[0]
    tm = min(tm, n)
    assert n % tm == 0
    d_att = N_HEADS * HEAD_DIM
    cos, sin = _rope_tables(pos)
    out_specs = [pl.BlockSpec((tm, 2 * d_att), lambda i: (i, 0))]
    out_shape = [jax.ShapeDtypeStruct((n, 2 * d_att), F32)]
    if dilated:
        for _, dil in GROUPS:
            assert tm % (16 * dil) == 0
            out_specs += [pl.BlockSpec((tm // dil, dil * d_att), lambda i: (i, 0))] * 3
            out_shape += [jax.ShapeDtypeStruct((n // dil, dil * d_att), BF16)] * 3
    else:
        out_specs.append(pl.BlockSpec((tm, N_GROUPS * d_att), lambda i: (i, 0)))
        out_shape.append(jax.ShapeDtypeStruct((n, N_GROUPS * d_att), BF16))
    return pl.pallas_call(
        functools.partial(_qkv_kernel, dilated=dilated),
        grid=(n // tm,),
        in_specs=[
            pl.BlockSpec((tm, D_MODEL), lambda i: (i, 0)),
            pl.BlockSpec((tm, LANES), lambda i: (i, 0)),
            pl.BlockSpec((tm, LANES), lambda i: (i, 0)),
            _full((D_MODEL, 2 * d_att)), _full((D_MODEL, N_GROUPS * d_att)),
        ],
        out_specs=out_specs,
        out_shape=out_shape,
        scratch_shapes=[pltpu.VMEM((d_att // LANES, tm, LANES), F32)] if dilated else [],
        compiler_params=_params(("arbitrary",)),
        name="qkv_rope",
    )(x, cos, sin, w_kv, w_q)


ATTN_STEP_ROWS = 512
ATTN_SUB_ROWS = 128


def _attn_prompt_kernel(q_ref, kp_ref, kc_ref, vp_ref, vc_ref, o_ref, lse_ref, kx_ref, vx_ref, *, rows, sub):
    blk = pl.program_id(1)
    back = KEYS_PER_GROUP
    span = back + sub
    kx_ref[0:back, :] = kp_ref[...]
    kx_ref[back:back + rows, :] = kc_ref[...]
    vx_ref[0:back, :] = vp_ref[...]
    vx_ref[back:back + rows, :] = vc_ref[...]
    a = lax.broadcasted_iota(jnp.int32, (sub, span), 0)
    c = lax.broadcasted_iota(jnp.int32, (sub, span), 1)
    band = (c >= a) & (c <= a + back)
    lane = lax.broadcasted_iota(jnp.int32, (sub, LANES), 1)
    low = lane < HEAD_DIM

    for u in range(rows // sub):
        r0 = u * sub
        valid = band & ((blk > 0) | (c + r0 >= back)) if r0 < back else band
        lse_all = jnp.zeros((sub, LANES), F32)
        for p in range(N_HEADS // 2):
            sl = slice(p * LANES, (p + 1) * LANES)
            qp = q_ref[pl.ds(r0, sub), sl]
            kk = kx_ref[pl.ds(r0, span), sl]
            vv = vx_ref[pl.ds(r0, span), sl]
            outs = []
            for j in range(2):
                qm = jnp.where(low if j == 0 else jnp.logical_not(low), qp, jnp.zeros_like(qp))
                s = lax.dot_general(qm, kk, (((1,), (1,)), ((), ())), preferred_element_type=F32)
                s = jnp.where(valid, s, NEG)
                m = jnp.max(s, axis=1, keepdims=True)
                e = jnp.exp(s - m)
                den = jnp.sum(e, axis=1, keepdims=True)
                outs.append(_dot(e.astype(BF16), vv) / den)
                lse_all = jnp.where(lane == 2 * p + j, m + jnp.log(den), lse_all)
            o_ref[pl.ds(r0, sub), sl] = jnp.where(low, outs[0], outs[1]).astype(BF16)
        lse_ref[0, pl.ds(r0, sub), :] = lse_all


def _attn_prompt_group(qg, kg, vg, g):
    dil = GROUPS[g][1]
    back = KEYS_PER_GROUP
    assert GROUPS[g][0] // dil == back
    d_att = N_HEADS * HEAD_DIM
    n_rows = kg.shape[0]
    rows = min(ATTN_STEP_ROWS, n_rows)
    sub = ATTN_SUB_ROWS
    assert kg.shape[1] == dil * d_att and n_rows % rows == 0 and rows % back == 0 and rows % sub == 0
    cur = pl.BlockSpec((rows, d_att), lambda r, b: (b, r))
    prev = pl.BlockSpec((back, d_att), lambda r, b: (jnp.maximum(b * (rows // back) - 1, 0), r))
    return pl.pallas_call(
        functools.partial(_attn_prompt_kernel, rows=rows, sub=sub),
        grid=(dil, n_rows // rows),
        in_specs=[cur, prev, cur, prev, cur],
        out_specs=[cur, pl.BlockSpec((1, rows, LANES), lambda r, b: (r, b, 0))],
        out_shape=[jax.ShapeDtypeStruct((n_rows, dil * d_att), BF16),
                   jax.ShapeDtypeStruct((dil, n_rows, LANES), F32)],
        scratch_shapes=[pltpu.VMEM((back + rows, d_att), BF16), pltpu.VMEM((back + rows, d_att), BF16)],
        compiler_params=_params(("arbitrary", "arbitrary")),
        name=f"attn_prompt_g{g}",
    )(qg, kg, kg, vg, vg)


SAMPLE_ROWS = 16


def _sample_valid(t_len, past):
    key = np.arange(past + LANES)
    valid = np.zeros((SAMPLE_ROWS, past + LANES), np.float32)
    for r in range(N_GROUPS * t_len):
        g, i = divmod(r, t_len)
        window, dil = GROUPS[g]
        dist = past + i - key
        ok = (dist >= 0) & (dist <= window) & (dist % dil == 0) & (key < past + t_len)
        assert ok.sum() == window // dil + 1
        valid[r] = ok
    return valid[:, :past], valid[:, past:]


def _group_reduce(x, op):
    quarter = x.shape[0] // 4
    return op(op(x, pltpu.roll(x, quarter, 0)), op(pltpu.roll(x, 2 * quarter, 0), pltpu.roll(x, 3 * quarter, 0)))


def _attn_sample_kernel(q_ref, kvn_ref, vc_ref, vn_ref, cache_ref, o_ref, *, t_len):
    d_att = N_HEADS * HEAD_DIM
    q = q_ref[0].astype(F32)
    head = lax.broadcasted_iota(jnp.int32, (N_HEADS, d_att), 0)
    lane = lax.broadcasted_iota(jnp.int32, (N_HEADS, d_att), 1)
    own = head == lane // HEAD_DIM
    blocks = []
    for r in range(SAMPLE_ROWS):
        g, i = divmod(r, t_len)
        if g < N_GROUPS:
            qr = q[i:i + 1, g * d_att:(g + 1) * d_att]
            blocks.append(jnp.where(own, jnp.broadcast_to(qr, (N_HEADS, d_att)), 0.0))
        else:
            blocks.append(jnp.zeros((N_HEADS, d_att), F32))
    qbd = jnp.concatenate(blocks, axis=0).astype(BF16)
    kvn = jnp.concatenate([kvn_ref[0], jnp.zeros((LANES - t_len, 2 * d_att), F32)], axis=0).astype(BF16)
    s_c = jnp.where(vc_ref[...] > 0.0, _dot(qbd, cache_ref[0, 0].astype(BF16)), NEG)
    s_n = jnp.where(vn_ref[...] > 0.0, lax.dot_general(qbd, kvn[:, :d_att], (((1,), (1,)), ((), ())),
                                                       preferred_element_type=F32), NEG)
    ones = jnp.ones((s_n.shape[0], LANES), F32)
    m = jnp.maximum(jnp.max(s_c, axis=1, keepdims=True), jnp.max(s_n, axis=1, keepdims=True))
    m = _group_reduce(m * ones, jnp.maximum)[:, 0:1]
    e_c, e_n = jnp.exp(s_c - m), jnp.exp(s_n - m)
    den = jnp.sum(e_c, axis=1, keepdims=True) + jnp.sum(e_n, axis=1, keepdims=True)
    den = _group_reduce(den * ones, jnp.add)[:, 0:1]
    o = lax.dot_general((e_c / den).astype(BF16), cache_ref[0, 1].astype(BF16), (((1,), (1,)), ((), ())),
                        preferred_element_type=F32)
    o = o + _dot((e_n / den).astype(BF16), kvn[:, d_att:])
    o = jnp.sum(jnp.where(own[None], o.reshape(SAMPLE_ROWS, N_HEADS, d_att), 0.0), axis=1)
    out = o[0:t_len]
    for g in range(1, N_GROUPS):
        out = out + o[g * t_len:(g + 1) * t_len]
    o_ref[0] = out


def _attn_sample(q, kv_new, cache_t):
    bsz, t_len, _ = q.shape
    past = cache_t.shape[-1]
    d_att = N_HEADS * HEAD_DIM
    assert N_GROUPS * t_len <= SAMPLE_ROWS and SAMPLE_ROWS % 4 == 0 and SAMPLE_ROWS // 4 == t_len
    assert PAST_LEN >= past and past >= max(w for w, _ in GROUPS)
    valid_c, valid_n = (np.repeat(v, N_HEADS, axis=0) for v in _sample_valid(t_len, past))
    return pl.pallas_call(
        functools.partial(_attn_sample_kernel, t_len=t_len),
        grid=(bsz,),
        in_specs=[
            pl.BlockSpec((1, t_len, N_GROUPS * d_att), lambda b: (b, 0, 0)),
            pl.BlockSpec((1, t_len, 2 * d_att), lambda b: (b, 0, 0)),
            _full(valid_c.shape), _full(valid_n.shape),
            pl.BlockSpec((1, 2, d_att, past), lambda b: (b, 0, 0, 0)),
        ],
        out_specs=pl.BlockSpec((1, t_len, d_att), lambda b: (b, 0, 0)),
        out_shape=jax.ShapeDtypeStruct((bsz, t_len, d_att), F32),
        compiler_params=_params(("arbitrary",)),
        name="attn_sample",
    )(q, kv_new, jnp.asarray(valid_c), jnp.asarray(valid_n), cache_t)


def _store_token_tiles(ref, x):
    n = x.shape[0]
    for s in range(TILE_ROWS):
        ref[pl.ds(s, n, stride=TILE_ROWS), :] = x[:, s * LANES:(s + 1) * LANES]


def _load_token_slab(ref, n, s):
    return ref[pl.ds(s, n, stride=TILE_ROWS), :]


def _token_rows(ref, t):
    return ref.at[pl.ds(pl.multiple_of(t * TILE_ROWS, TILE_ROWS), TILE_ROWS), :]


def _split3(w):
    hi = w.astype(BF16)
    r1 = w - hi.astype(F32)
    mid = r1.astype(BF16)
    lo = (r1 - mid.astype(F32)).astype(BF16)
    return hi, mid, lo


def _oproj_router_tail(att, x_ref, wo_ref, g_ref, b_ref, wrh_ref, wrl_ref, x3t_ref, route_ref, cnt_ref):
    tm = att.shape[0]

    @pl.when(pl.program_id(0) == 0)
    def _():
        cnt_ref[...] = jnp.zeros_like(cnt_ref)

    x3 = _ln(ALPHA * x_ref[...] + _dot(att.astype(BF16), wo_ref[...]), g_ref[...], b_ref[...])
    _store_token_tiles(x3t_ref, x3)
    xh = x3.astype(BF16)
    xl = (x3 - xh.astype(F32)).astype(BF16)
    logits = _dot(xh, wrh_ref[...]) + (_dot(xh, wrl_ref[...]) + _dot(xl, wrh_ref[...]))
    lane = lax.broadcasted_iota(jnp.int32, logits.shape, 1)
    logits = jnp.where(lane < N_EXPERTS, logits, NEG)
    v1 = jnp.max(logits, axis=1, keepdims=True)
    i1 = jnp.min(jnp.where(logits == v1, lane, LANES), axis=1, keepdims=True)
    rest = jnp.where(lane == i1, NEG, logits)
    v2 = jnp.max(rest, axis=1, keepdims=True)
    i2 = jnp.min(jnp.where(rest == v2, lane, LANES), axis=1, keepdims=True)
    e2 = jnp.exp(v2 - v1)
    den = 1.0 + e2
    sel = jnp.where((lane == i1) | (lane == i2), 1.0, 0.0)
    tri = jnp.where(lax.broadcasted_iota(jnp.int32, (tm, tm), 1) < lax.broadcasted_iota(jnp.int32, (tm, tm), 0),
                    1.0, 0.0).astype(BF16)
    before = _dot(tri, sel.astype(BF16)) + cnt_ref[0:1, :]
    rank1 = jnp.sum(jnp.where(lane == i1, before, 0.0), axis=1, keepdims=True)
    rank2 = jnp.sum(jnp.where(lane == i2, before, 0.0), axis=1, keepdims=True)
    cnt_ref[0:1, :] = cnt_ref[0:1, :] + jnp.sum(sel, axis=0, keepdims=True)
    fields = (i1.astype(F32), i2.astype(F32), 1.0 / den, e2 / den, rank1, rank2)
    route = jnp.zeros(logits.shape, F32)
    for k, val in enumerate(fields):
        route = jnp.where(lane == k, val, route)
    route_ref[...] = route


def _load_dilated(o_ref, lse_ref, o_scr, l_scr, dil):
    if dil == 1:
        return o_ref[...].astype(F32), lse_ref[0]
    d_att = N_HEADS * HEAD_DIM
    rows = o_ref.shape[0]
    n_slabs = d_att // LANES
    for r in range(dil):
        for c in range(n_slabs):
            col = r * d_att + c * LANES
            o_scr[c, pl.ds(r, rows, stride=dil), :] = o_ref[:, col:col + LANES].astype(F32)
        l_scr[pl.ds(r, rows, stride=dil), :] = lse_ref[r]
    return jnp.concatenate([o_scr[c] for c in range(n_slabs)], axis=1), l_scr[...]


def _oproj_prompt_kernel(o0_ref, l0_ref, o1_ref, l1_ref, o2_ref, l2_ref, ex_ref, x_ref, wo_ref, g_ref, b_ref,
                         wrh_ref, wrl_ref, x3t_ref, route_ref, cnt_ref, *scr):
    parts = [_load_dilated(o_ref, l_ref, scr[2 * g], scr[2 * g + 1], GROUPS[g][1])
             for g, (o_ref, l_ref) in enumerate(((o0_ref, l0_ref), (o1_ref, l1_ref), (o2_ref, l2_ref)))]
    lse = [l for _, l in parts]
    m = jnp.maximum(jnp.maximum(lse[0], lse[1]), lse[2])
    w = [jnp.exp(l - m) for l in lse]
    den = w[0] + w[1] + w[2]
    att = None
    for g, (o, _) in enumerate(parts):
        hi, mid, lo = _split3(w[g] / den)
        wexp = _dot(hi, ex_ref[...]) + (_dot(mid, ex_ref[...]) + _dot(lo, ex_ref[...]))
        term = wexp * o
        att = term if att is None else att + term
    _oproj_router_tail(att, x_ref, wo_ref, g_ref, b_ref, wrh_ref, wrl_ref, x3t_ref, route_ref, cnt_ref)


def _oproj_sample_kernel(att_ref, x_ref, wo_ref, g_ref, b_ref, wrh_ref, wrl_ref, x3t_ref, route_ref, cnt_ref):
    _oproj_router_tail(att_ref[...], x_ref, wo_ref, g_ref, b_ref, wrh_ref, wrl_ref, x3t_ref, route_ref, cnt_ref)


def _router_split(w_router):
    wr = jnp.zeros((D_MODEL, LANES), F32).at[:, :N_EXPERTS].set(w_router)
    hi = wr.astype(BF16)
    return hi, (wr - hi.astype(F32)).astype(BF16)


def _oproj_router(att_parts, x, w_o, g, b, w_router, *, tm=256):
    n = x.shape[0]
    tm = min(tm, n)
    assert n % tm == 0
    d_att = N_HEADS * HEAD_DIM
    wrh, wrl = _router_split(w_router)
    row_blk = lambda w: pl.BlockSpec((tm, w), lambda i: (i, 0))
    tail_specs = [row_blk(D_MODEL), _full((d_att, D_MODEL)), _full((1, D_MODEL)), _full((1, D_MODEL)),
                  _full((D_MODEL, LANES)), _full((D_MODEL, LANES))]
    tail_args = (x, w_o, g, b, wrh, wrl)
    scratch = []
    if len(att_parts) == 1:
        kern, specs, args = _oproj_sample_kernel, [row_blk(d_att)], tuple(att_parts)
    else:
        expand = (np.arange(LANES)[:, None] == np.arange(d_att)[None, :] // HEAD_DIM).astype(np.float32)
        kern, specs, args = _oproj_prompt_kernel, [], []
        for (_, dil), (o, lse) in zip(GROUPS, att_parts):
            specs += [pl.BlockSpec((tm // dil, dil * d_att), lambda i: (i, 0)),
                      pl.BlockSpec((dil, tm // dil, LANES), lambda i: (0, i, 0))]
            args += [o, lse]
            scratch += [pltpu.VMEM((d_att // LANES, tm, LANES), F32), pltpu.VMEM((tm, LANES), F32)]
        specs.append(_full((LANES, d_att)))
        args.append(jnp.asarray(expand, BF16))
    return pl.pallas_call(
        kern,
        grid=(n // tm,),
        in_specs=specs + tail_specs,
        out_specs=[pl.BlockSpec((tm * TILE_ROWS, LANES), lambda i: (i, 0)), row_blk(LANES), _full((TILE_ROWS, LANES))],
        out_shape=[jax.ShapeDtypeStruct((n * TILE_ROWS, LANES), F32), jax.ShapeDtypeStruct((n, LANES), F32),
                   jax.ShapeDtypeStruct((TILE_ROWS, LANES), F32)],
        scratch_shapes=scratch,
        compiler_params=_params(("arbitrary",)),
        name="oproj_router",
    )(*args, *tail_args)


def _route_plan(routes, counts, tile, n_tiles):
    cnt = [c[0, :N_EXPERTS].astype(jnp.int32) for c in counts]
    total = sum(cnt)
    padded = (total + tile - 1) // tile * tile
    ends = jnp.cumsum(padded)
    offs = ends - padded
    fill = jnp.concatenate([offs + total, ends]).astype(jnp.int32)
    pos = []
    for route, c in zip(routes, cnt):
        e = route[:, 0:2].astype(jnp.int32)
        pos.append((offs[e] + route[:, 4:6].astype(jnp.int32)).reshape(-1))
        offs = offs + c
    n_live = (ends[-1] // tile).astype(jnp.int32)
    tile_id = jnp.minimum(jnp.arange(n_tiles, dtype=jnp.int32), n_live - 1)
    tile_expert = jnp.sum((ends // tile)[None, :] <= tile_id[:, None], axis=1).astype(jnp.int32)
    return pos, tile_expert, n_live.reshape(1), fill


def _dispatch_kernel(pos_ref, fill_ref, xa_ref, xb_ref, xs_ref, z_ref, sem, *, tm, steps_a, n_b, tile, n_tiles):
    i = pl.program_id(0)

    def scatter(src_ref, count, first_token):
        def row_copy(r, s):
            dst = _token_rows(xs_ref, pos_ref[(first_token + r) * 2 + s])
            return pltpu.make_async_copy(_token_rows(src_ref, r), dst, sem)

        def issue(r, c):
            row_copy(r, 0).start(priority=0)
            row_copy(r, 1).start(priority=1)
            return c

        def drain(r, c):
            row_copy(r, 0).wait()
            row_copy(r, 1).wait()
            return c

        lax.fori_loop(0, count, issue, 0)
        lax.fori_loop(0, count, drain, 0)

    @pl.when(i < steps_a)
    def _():
        scatter(xa_ref, tm, i * tm)

    @pl.when(i == steps_a)
    def _():
        scatter(xb_ref, n_b, steps_a * tm)
        z_ref[...] = jnp.zeros_like(z_ref)
        zero_row = lambda r: pltpu.make_async_copy(z_ref.at[0:TILE_ROWS, :], _token_rows(xs_ref, r), sem)
        tile_rows = tile * TILE_ROWS
        zero_tile = lambda j: pltpu.make_async_copy(
            z_ref, xs_ref.at[pl.ds(pl.multiple_of(j * tile_rows, tile_rows), tile_rows), :], sem)

        def loop(lo, hi, copy):
            def issue(r, c):
                copy(r).start()
                return c

            def drain(r, c):
                copy(r).wait()
                return c

            lax.fori_loop(lo, hi, issue, 0)
            lax.fori_loop(lo, hi, drain, 0)

        for e in range(N_EXPERTS):
            loop(fill_ref[e], fill_ref[N_EXPERTS + e], zero_row)
        loop(fill_ref[2 * N_EXPERTS - 1] // tile, n_tiles, zero_tile)


def _dispatch(pos, fill_rng, x3t_a, x3t_b, n_tiles, tile, *, tm=256):
    n_a, n_b = x3t_a.shape[0] // TILE_ROWS, x3t_b.shape[0] // TILE_ROWS
    tm = min(tm, n_a)
    assert n_a % tm == 0
    steps_a = n_a // tm
    return pl.pallas_call(
        functools.partial(_dispatch_kernel, tm=tm, steps_a=steps_a, n_b=n_b, tile=tile, n_tiles=n_tiles),
        grid_spec=pltpu.PrefetchScalarGridSpec(
            num_scalar_prefetch=2, grid=(steps_a + 1,),
            in_specs=[pl.BlockSpec((tm * TILE_ROWS, LANES), lambda i, *_: (jnp.minimum(i, steps_a - 1), 0)),
                      pl.BlockSpec((n_b * TILE_ROWS, LANES), lambda i, *_: (0, 0))],
            out_specs=pl.BlockSpec(memory_space=pl.ANY),
            scratch_shapes=[pltpu.VMEM((tile * TILE_ROWS, LANES), F32), pltpu.SemaphoreType.DMA(())]),
        out_shape=jax.ShapeDtypeStruct((n_tiles * tile * TILE_ROWS, LANES), F32),
        compiler_params=_params(("arbitrary",)),
        name="moe_dispatch",
    )(pos, fill_rng, x3t_a, x3t_b)


def _moe_ffn_kernel(te_ref, nl_ref, xs_ref, wg_ref, wu_ref, wd_ref, ys_ref, acc_ref, xb_ref, *, tile):
    j, f = pl.program_id(0), pl.program_id(1)

    @pl.when(j < nl_ref[0])
    def _():
        @pl.when(f == 0)
        def _():
            for s in range(TILE_ROWS):
                xb_ref[:, s * LANES:(s + 1) * LANES] = _load_token_slab(xs_ref, tile, s).astype(BF16)

        xb = xb_ref[...]
        h = _silu(_dot(xb, wg_ref[...])) * _dot(xb, wu_ref[...])
        y = _dot(h.astype(BF16), wd_ref[...])

        @pl.when(f == 0)
        def _():
            acc_ref[...] = y

        @pl.when(f > 0)
        def _():
            acc_ref[...] += y

        @pl.when(f == pl.num_programs(1) - 1)
        def _():
            _store_token_tiles(ys_ref, acc_ref[...])

    @pl.when((j >= nl_ref[0]) & (f == pl.num_programs(1) - 1))
    def _():
        ys_ref[...] = jnp.zeros_like(ys_ref)


def _moe_ffn(xs, tile_expert, n_live, wg, wu, wd, *, tile, n_f=2):
    n_tiles = tile_expert.shape[0]
    d_ff = wg.shape[2]
    tf = d_ff // n_f
    assert d_ff % n_f == 0 and tf % LANES == 0 and xs.shape[0] == n_tiles * tile * TILE_ROWS
    row_map = lambda j, f, te, nl: (jnp.minimum(j, nl[0] - 1), 0)
    f_idx = lambda j, f, nl: jnp.where(j < nl[0], f, n_f - 1)
    return pl.pallas_call(
        functools.partial(_moe_ffn_kernel, tile=tile),
        grid_spec=pltpu.PrefetchScalarGridSpec(
            num_scalar_prefetch=2, grid=(n_tiles, n_f),
            in_specs=[
                pl.BlockSpec((tile * TILE_ROWS, LANES), row_map),
                pl.BlockSpec((None, D_MODEL, tf), lambda j, f, te, nl: (te[j], 0, f_idx(j, f, nl))),
                pl.BlockSpec((None, D_MODEL, tf), lambda j, f, te, nl: (te[j], 0, f_idx(j, f, nl))),
                pl.BlockSpec((None, tf, D_MODEL), lambda j, f, te, nl: (te[j], f_idx(j, f, nl), 0)),
            ],
            out_specs=pl.BlockSpec((tile * TILE_ROWS, LANES), lambda j, f, te, nl: (j, 0)),
            scratch_shapes=[pltpu.VMEM((tile, D_MODEL), F32), pltpu.VMEM((tile, D_MODEL), BF16)]),
        out_shape=jax.ShapeDtypeStruct(xs.shape, F32),
        compiler_params=_params(("arbitrary", "arbitrary")),
        name="moe_ffn",
    )(tile_expert, n_live, xs, wg, wu, wd)


def _moe_combine_kernel(pos_ref, route_ref, x3t_ref, ys_ref, g_ref, b_ref, o_ref, y_ref, pre_ref, sem, *, tm):
    i, n = pl.program_id(0), pl.num_programs(0)
    buf = i % 2

    def each_row(step, which, act):
        def body(r, c):
            for s in range(2):
                src = _token_rows(ys_ref, pos_ref[(step * tm + r) * 2 + s])
                dst = y_ref.at[which, s, pl.ds(pl.multiple_of(r * TILE_ROWS, TILE_ROWS), TILE_ROWS), :]
                act(pltpu.make_async_copy(src, dst, sem.at[which]))
            return c

        lax.fori_loop(0, tm, body, 0)

    start = lambda copy: copy.start()
    wait = lambda copy: copy.wait()

    @pl.when(i == 0)
    def _():
        each_row(0, 0, start)

    @pl.when(i + 1 < n)
    def _():
        each_row(i + 1, 1 - buf, start)

    each_row(i, buf, wait)
    route = route_ref[...]
    g1, g2 = route[:, 2:3], route[:, 3:4]
    for s in range(TILE_ROWS):
        y = (g1 * y_ref[buf, 0, pl.ds(s, tm, stride=TILE_ROWS), :]
             + g2 * y_ref[buf, 1, pl.ds(s, tm, stride=TILE_ROWS), :])
        pre_ref[:, s * LANES:(s + 1) * LANES] = ALPHA * _load_token_slab(x3t_ref, tm, s) + y
    o_ref[...] = _ln(pre_ref[...], g_ref[...], b_ref[...])


def _moe_combine(pos, route, x3t, ys, g, b, *, tm=256):
    n = route.shape[0]
    tm = min(tm, n)
    assert n % tm == 0
    return pl.pallas_call(
        functools.partial(_moe_combine_kernel, tm=tm),
        grid_spec=pltpu.PrefetchScalarGridSpec(
            num_scalar_prefetch=1, grid=(n // tm,),
            in_specs=[
                pl.BlockSpec((tm, LANES), lambda i, *_: (i, 0)),
                pl.BlockSpec((tm * TILE_ROWS, LANES), lambda i, *_: (i, 0)),
                pl.BlockSpec(memory_space=pl.ANY),
                pl.BlockSpec((1, D_MODEL), lambda i, *_: (0, 0)),
                pl.BlockSpec((1, D_MODEL), lambda i, *_: (0, 0)),
            ],
            out_specs=pl.BlockSpec((tm, D_MODEL), lambda i, *_: (i, 0)),
            scratch_shapes=[pltpu.VMEM((2, 2, tm * TILE_ROWS, LANES), F32), pltpu.VMEM((tm, D_MODEL), F32),
                            pltpu.SemaphoreType.DMA((2,))]),
        out_shape=jax.ShapeDtypeStruct((n, D_MODEL), F32),
        compiler_params=_params(("arbitrary",)),
        name="moe_combine",
    )(pos, route, x3t, ys, g, b)


def kernel(x_prompt, x_sample, cache_kv, state_conv, conv_w_in, conv_b_in, conv_w_dw, conv_b_dw, conv_ln_g,
           conv_ln_b, conv_w_out, conv_b_out, ffn_w_gate, ffn_w_up, ffn_w_down, w_kv, attn_w_q, attn_w_o,
           moe_w_router, moe_w_gate, moe_w_up, moe_w_down, ln_g, ln_b):
    bp, t_p, _ = x_prompt.shape
    bs, t_s, _ = x_sample.shape
    assert bp == 1 and conv_w_in.shape[0] == 1 and attn_w_q.shape[0] == 1
    d_att = N_HEADS * HEAD_DIM
    row = lambda a: a.reshape(1, -1)
    bf = lambda a: a.astype(BF16)

    conv_w = (bf(conv_w_in[0]), row(conv_b_in[0]), conv_w_dw[0], row(conv_b_dw[0]), row(conv_ln_g[0]),
              row(conv_ln_b[0]), bf(conv_w_out[0]), row(conv_b_out[0]), row(ln_g[0, 0]), row(ln_b[0, 0]))
    ffn_w = (bf(ffn_w_gate[0]), bf(ffn_w_up[0]), bf(ffn_w_down[0]), row(ln_g[0, 1]), row(ln_b[0, 1]))
    wkv_b, wq_b, wo_b = bf(w_kv), bf(attn_w_q[0]), bf(attn_w_o[0])
    ln10 = (row(ln_g[1, 0]), row(ln_b[1, 0]))
    ln11 = (row(ln_g[1, 1]), row(ln_b[1, 1]))

    x1, conv_p = _conv_layer_prompt(x_prompt, jnp.zeros((bp, HALO, D_MODEL), F32), *conv_w)
    x2 = _ffn_layer(x1.reshape(t_p, D_MODEL), *ffn_w)
    kv_p, *qkv_views = _qkv_layer(x2, jnp.arange(t_p, dtype=jnp.int32), wkv_b, wq_b, dilated=True)
    parts = [_attn_prompt_group(*qkv_views[3 * g:3 * g + 3], g) for g in range(N_GROUPS)]
    x3t_p, route_p, cnt_p = _oproj_router(parts, x2, wo_b, *ln10, moe_w_router[0])
    n_keep = min(max(w for w, _ in GROUPS), t_p)
    kv_prompt = kv_p[t_p - n_keep:].reshape(bp, n_keep, 2, N_HEADS, HEAD_DIM)

    n_s = bs * t_s
    x1s, conv_s = _conv_layer_sample(x_sample, state_conv[0], *conv_w)
    x2s = _ffn_layer(x1s.reshape(n_s, D_MODEL), *ffn_w)
    pos_s = PAST_LEN + jnp.tile(jnp.arange(t_s, dtype=jnp.int32), bs)
    kv_s, q_s = _qkv_layer(x2s, pos_s, wkv_b, wq_b, dilated=False)
    att_s = _attn_sample(q_s.reshape(bs, t_s, N_GROUPS * d_att), kv_s.reshape(bs, t_s, 2 * d_att),
                         jnp.transpose(cache_kv, (0, 2, 3, 4, 1)).reshape(bs, 2, d_att, cache_kv.shape[1]))
    x3t_s, route_s, cnt_s = _oproj_router((att_s.reshape(n_s, d_att),), x2s, wo_b, *ln10, moe_w_router[0])
    kv_sample = kv_s.reshape(bs, t_s, 2, N_HEADS, HEAD_DIM)

    n_slots = 2 * (t_p + n_s)
    n_tiles = (n_slots + N_EXPERTS * (MOE_TILE - 1)) // MOE_TILE
    (pos_p, pos_sm), tile_expert, n_live, fill_rng = _route_plan((route_p, route_s), (cnt_p, cnt_s), MOE_TILE, n_tiles)
    xs = _dispatch(jnp.concatenate([pos_p, pos_sm]), fill_rng, x3t_p, x3t_s, n_tiles, MOE_TILE)
    ys = _moe_ffn(xs, tile_expert, n_live, bf(moe_w_gate[0]), bf(moe_w_up[0]), bf(moe_w_down[0]), tile=MOE_TILE)
    y_prompt = _moe_combine(pos_p, route_p, x3t_p, ys, *ln11).reshape(bp, t_p, D_MODEL)
    y_sample = _moe_combine(pos_sm, route_s, x3t_s, ys, *ln11).reshape(bs, t_s, D_MODEL)

    return (y_prompt, y_sample, conv_p[None], conv_s[None], kv_prompt, kv_sample)
```

```python
import functools

import numpy as np
import jax
import jax.numpy as jnp
from jax import lax
from jax.experimental import pallas as pl
from jax.experimental.pallas import tpu as pltpu

F32 = jnp.float32
BF16 = jnp.bfloat16

D_MODEL = 1024
N_HEADS = 16
HEAD_DIM = 64
GROUPS = ((128, 1), (512, 4), (2048, 16))
N_GROUPS = len(GROUPS)
KEYS_PER_GROUP = 128
CONV_WIDTH = 31
HALO = CONV_WIDTH - 1
N_EXPERTS = 8
ROPE_THETA = 10000.0
LN_EPS = 1e-5
DEPTH = 2
ALPHA = (2 * DEPTH) ** 0.25
PAST_LEN = 16384
NEG = -1e30

LANES = 128
HALO_PAD = 32
TILE_ROWS = 8
MOE_TILE = 512
VMEM_LIMIT = 56 * 1024 * 1024


def _params(sem, vmem=VMEM_LIMIT):
    return pltpu.CompilerParams(dimension_semantics=sem, vmem_limit_bytes=vmem)


def _dot(a, b):
    return jnp.dot(a, b, preferred_element_type=F32)


def _sigmoid(x):
    return 1.0 / (1.0 + jnp.exp(-x))


def _silu(x):
    return x * _sigmoid(x)


def _ln(x, g, b):
    xc = x - jnp.mean(x, axis=-1, keepdims=True)
    var = jnp.mean(xc * xc, axis=-1, keepdims=True)
    return xc * lax.rsqrt(var + LN_EPS) * g + b


def _full(shape):
    return pl.BlockSpec(shape, lambda *_: (0,) * len(shape))


def _conv_taps(s_ref, wdw_ref, bdw_ref, c_ref, sh_ref, n_rows, row_chunk):
    base = HALO_PAD - HALO
    phases = [[k for k in range(CONV_WIDTH) if (base + k) % TILE_ROWS == ph] for ph in range(TILE_ROWS)]

    def chunk(rc, carry):
        r0 = pl.multiple_of(rc * row_chunk, row_chunk)
        win = s_ref.at[pl.ds(r0, row_chunk + HALO_PAD), :]
        for c0 in range(0, D_MODEL, LANES):
            for ph, taps in enumerate(phases):
                n_ext = row_chunk + taps[-1] - taps[0]
                sh_ref[ph, 0:n_ext, :] = win[base + taps[0]:base + taps[0] + n_ext, c0:c0 + LANES]
            acc = jnp.broadcast_to(bdw_ref[:, c0:c0 + LANES], (row_chunk, LANES))
            for ph, taps in enumerate(phases):
                for k in taps:
                    acc = acc + wdw_ref[k:k + 1, c0:c0 + LANES] * sh_ref[ph, k - taps[0]:k - taps[0] + row_chunk, :]
            c_ref[pl.ds(r0, row_chunk), c0:c0 + LANES] = acc
        return carry

    lax.fori_loop(0, n_rows // row_chunk, chunk, 0)


def _conv_prompt_kernel(x_ref, st_ref, win_ref, bin_ref, wdw_ref, bdw_ref, cg_ref, cb_ref, wout_ref, bout_ref,
                        lg_ref, lb_ref, o_ref, ns_ref, s_ref, c_ref, sh_ref, *, tm, row_chunk):
    base = HALO_PAD - HALO

    @pl.when(pl.program_id(1) == 0)
    def _():
        s_ref[0:HALO_PAD, :] = jnp.zeros((HALO_PAD, D_MODEL), F32)
        s_ref[base:HALO_PAD, :] = st_ref[0]

    x = x_ref[0]
    h = _dot(x.astype(BF16), win_ref[...]) + bin_ref[...]
    s_ref[HALO_PAD:HALO_PAD + tm, :] = h[:, :D_MODEL] * _sigmoid(h[:, D_MODEL:])
    _conv_taps(s_ref, wdw_ref, bdw_ref, c_ref, sh_ref, tm, row_chunk)
    c = _silu(_ln(c_ref[...], cg_ref[...], cb_ref[...]))
    mix = _dot(c.astype(BF16), wout_ref[...]) + bout_ref[...]
    o_ref[0] = _ln(ALPHA * x + mix, lg_ref[...], lb_ref[...])
    tail = s_ref[pl.ds(tm + base, HALO), :]
    s_ref[base:HALO_PAD, :] = tail
    ns_ref[0] = tail


def _conv_layer_prompt(x, state, w_in, b_in, w_dw, b_dw, cg, cb, w_out, b_out, lg, lb, *, tm=256):
    bsz, t_len, _ = x.shape
    tm = min(tm, t_len)
    assert t_len % tm == 0 and tm >= HALO_PAD and tm % 8 == 0
    row_chunk = 64 if tm % 64 == 0 else 8
    row = lambda n: _full((1, n))
    return pl.pallas_call(
        functools.partial(_conv_prompt_kernel, tm=tm, row_chunk=row_chunk),
        grid=(bsz, t_len // tm),
        in_specs=[
            pl.BlockSpec((1, tm, D_MODEL), lambda b, i: (b, i, 0)),
            pl.BlockSpec((1, HALO, D_MODEL), lambda b, i: (b, 0, 0)),
            _full((D_MODEL, 2 * D_MODEL)), row(2 * D_MODEL),
            _full((CONV_WIDTH, D_MODEL)), row(D_MODEL), row(D_MODEL), row(D_MODEL),
            _full((D_MODEL, D_MODEL)), row(D_MODEL), row(D_MODEL), row(D_MODEL),
        ],
        out_specs=[
            pl.BlockSpec((1, tm, D_MODEL), lambda b, i: (b, i, 0)),
            pl.BlockSpec((1, HALO, D_MODEL), lambda b, i: (b, 0, 0)),
        ],
        out_shape=[jax.ShapeDtypeStruct(x.shape, F32), jax.ShapeDtypeStruct((bsz, HALO, D_MODEL), F32)],
        scratch_shapes=[pltpu.VMEM((HALO_PAD + tm, D_MODEL), F32), pltpu.VMEM((tm, D_MODEL), F32),
                        pltpu.VMEM((TILE_ROWS, row_chunk + HALO_PAD, LANES), F32)],
        compiler_params=_params(("arbitrary", "arbitrary")),
        name="conv_layer_prompt",
    )(x, state, w_in, b_in, w_dw, b_dw, cg, cb, w_out, b_out, lg, lb)


def _conv_sample_kernel(x_ref, st_ref, win_ref, bin_ref, wdw_ref, bdw_ref, cg_ref, cb_ref, wout_ref, bout_ref,
                        lg_ref, lb_ref, o_ref, ns_ref, s_ref, u_ref, c_ref, *, bsz, t_len):
    base = HALO_PAD - HALO
    x = x_ref[...]
    h = _dot(x.astype(BF16), win_ref[...]) + bin_ref[...]
    u_ref[...] = h[:, :D_MODEL] * _sigmoid(h[:, D_MODEL:])
    s_ref[:, 0:HALO_PAD, :] = jnp.zeros((bsz, HALO_PAD, D_MODEL), F32)
    s_ref[:, base:HALO_PAD, :] = st_ref[...]
    for b in range(bsz):
        s_ref[b, HALO_PAD:HALO_PAD + t_len, :] = u_ref[b * t_len:(b + 1) * t_len, :]
    acc = jnp.broadcast_to(bdw_ref[...][None], (bsz, t_len, D_MODEL))
    for k in range(CONV_WIDTH):
        acc = acc + wdw_ref[k:k + 1, :][None] * s_ref[:, base + k:base + k + t_len, :]
    for b in range(bsz):
        c_ref[b * t_len:(b + 1) * t_len, :] = acc[b]
    c = _silu(_ln(c_ref[...], cg_ref[...], cb_ref[...]))
    mix = _dot(c.astype(BF16), wout_ref[...]) + bout_ref[...]
    o_ref[...] = _ln(ALPHA * x + mix, lg_ref[...], lb_ref[...])
    ns_ref[...] = s_ref[:, base + t_len:HALO_PAD + t_len, :]


def _conv_layer_sample(x, state, w_in, b_in, w_dw, b_dw, cg, cb, w_out, b_out, lg, lb):
    bsz, t_len, _ = x.shape
    n = bsz * t_len
    assert t_len <= 8
    out, ns = pl.pallas_call(
        functools.partial(_conv_sample_kernel, bsz=bsz, t_len=t_len),
        out_shape=[jax.ShapeDtypeStruct((n, D_MODEL), F32), jax.ShapeDtypeStruct((bsz, HALO, D_MODEL), F32)],
        scratch_shapes=[pltpu.VMEM((bsz, HALO_PAD + 8, D_MODEL), F32), pltpu.VMEM((n, D_MODEL), F32),
                        pltpu.VMEM((n, D_MODEL), F32)],
        compiler_params=pltpu.CompilerParams(vmem_limit_bytes=VMEM_LIMIT),
        name="conv_layer_sample",
    )(x.reshape(n, D_MODEL), state, w_in, b_in, w_dw, b_dw, cg, cb, w_out, b_out, lg, lb)
    return out.reshape(x.shape), ns


def _ffn_kernel(x_ref, wg_ref, wu_ref, wd_ref, g_ref, b_ref, o_ref, acc_ref, xb_ref):
    f = pl.program_id(1)

    @pl.when(f == 0)
    def _():
        acc_ref[...] = jnp.zeros_like(acc_ref)
        xb_ref[...] = x_ref[...].astype(BF16)

    xb = xb_ref[...]
    h = _silu(_dot(xb, wg_ref[...])) * _dot(xb, wu_ref[...])
    acc_ref[...] += _dot(h.astype(BF16), wd_ref[...])

    @pl.when(f == pl.num_programs(1) - 1)
    def _():
        o_ref[...] = _ln(ALPHA * x_ref[...] + acc_ref[...], g_ref[...], b_ref[...])


def _ffn_layer(x, wg, wu, wd, g, b, *, tm=512, n_f=2):
    n = x.shape[0]
    d_ff = wg.shape[1]
    tm = min(tm, n)
    tf = d_ff // n_f
    assert n % tm == 0 and d_ff % n_f == 0 and tf % LANES == 0
    return pl.pallas_call(
        _ffn_kernel,
        grid=(n // tm, n_f),
        in_specs=[
            pl.BlockSpec((tm, D_MODEL), lambda i, f: (i, 0)),
            pl.BlockSpec((D_MODEL, tf), lambda i, f: (0, f)),
            pl.BlockSpec((D_MODEL, tf), lambda i, f: (0, f)),
            pl.BlockSpec((tf, D_MODEL), lambda i, f: (f, 0)),
            _full((1, D_MODEL)), _full((1, D_MODEL)),
        ],
        out_specs=pl.BlockSpec((tm, D_MODEL), lambda i, f: (i, 0)),
        out_shape=jax.ShapeDtypeStruct((n, D_MODEL), F32),
        scratch_shapes=[pltpu.VMEM((tm, D_MODEL), F32), pltpu.VMEM((tm, D_MODEL), BF16)],
        compiler_params=_params(("arbitrary", "arbitrary")),
        name="ffn_dense",
    )(x, wg, wu, wd, g, b)


def _rope_tables(pos):
    half = HEAD_DIM // 2
    inv = 1.0 / (ROPE_THETA ** (jnp.arange(half, dtype=F32) * (2.0 / HEAD_DIM)))
    ang = pos.astype(F32)[:, None] * inv[None, :]
    cos, sin = jnp.cos(ang), jnp.sin(ang)
    return jnp.tile(cos, (1, 4)), jnp.tile(jnp.concatenate([-sin, sin], axis=1), (1, 2))


def _rope(x, cos, sin, first_half):
    half = HEAD_DIM // 2
    out = []
    for c0 in range(0, x.shape[1], LANES):
        slab = x[:, c0:c0 + LANES]
        partner = jnp.where(first_half, pltpu.roll(slab, LANES - half, 1), pltpu.roll(slab, half, 1))
        out.append(slab * cos + partner * sin)
    return jnp.concatenate(out, axis=1)


def _store_dilated(dst_ref, val, s_ref, dil):
    if dil == 1:
        dst_ref[...] = val.astype(BF16)
        return
    d_att = N_HEADS * HEAD_DIM
    rows = val.shape[0] // dil
    for c in range(d_att // LANES):
        s_ref[c] = val[:, c * LANES:(c + 1) * LANES]
    for r in range(dil):
        for c in range(d_att // LANES):
            col = r * d_att + c * LANES
            dst_ref[:, col:col + LANES] = s_ref[c, pl.ds(r, rows, stride=dil), :].astype(BF16)


def _qkv_kernel(x_ref, cos_ref, sin_ref, wkv_ref, wq_ref, kv_ref, *rest, dilated):
    xb = x_ref[...].astype(BF16)
    cos, sin = cos_ref[...], sin_ref[...]
    first_half = (lax.broadcasted_iota(jnp.int32, cos.shape, 1) % HEAD_DIM) < HEAD_DIM // 2
    d_att = N_HEADS * HEAD_DIM
    kv = _dot(xb, wkv_ref[...])
    k = _rope(kv[:, :d_att], cos, sin, first_half)
    v = kv[:, d_att:]
    kv_ref[:, :d_att] = k
    kv_ref[:, d_att:] = v
    q = _rope(_dot(xb, wq_ref[...]), cos, sin, first_half) * (HEAD_DIM ** -0.5)
    if not dilated:
        rest[0][...] = q.astype(BF16)
        return
    s_ref = rest[-1]
    for g, (_, dil) in enumerate(GROUPS):
        q_ref, k_ref, v_ref = rest[3 * g:3 * g + 3]
        _store_dilated(q_ref, q[:, g * d_att:(g + 1) * d_att], s_ref, dil)
        _store_dilated(k_ref, k, s_ref, dil)
        _store_dilated(v_ref, v, s_ref, dil)


def _qkv_layer(x, pos, w_kv, w_q, *, dilated, tm=256):
    n = x.shape[0]
    tm = min(tm, n)
    assert n % tm == 0
    d_att = N_HEADS * HEAD_DIM
    cos, sin = _rope_tables(pos)
    out_specs = [pl.BlockSpec((tm, 2 * d_att), lambda i: (i, 0))]
    out_shape = [jax.ShapeDtypeStruct((n, 2 * d_att), F32)]
    if dilated:
        for _, dil in GROUPS:
            assert tm % (16 * dil) == 0
            out_specs += [pl.BlockSpec((tm // dil, dil * d_att), lambda i: (i, 0))] * 3
            out_shape += [jax.ShapeDtypeStruct((n // dil, dil * d_att), BF16)] * 3
    else:
        out_specs.append(pl.BlockSpec((tm, N_GROUPS * d_att), lambda i: (i, 0)))
        out_shape.append(jax.ShapeDtypeStruct((n, N_GROUPS * d_att), BF16))
    return pl.pallas_call(
        functools.partial(_qkv_kernel, dilated=dilated),
        grid=(n // tm,),
        in_specs=[
            pl.BlockSpec((tm, D_MODEL), lambda i: (i, 0)),
            pl.BlockSpec((tm, LANES), lambda i: (i, 0)),
            pl.BlockSpec((tm, LANES), lambda i: (i, 0)),
            _full((D_MODEL, 2 * d_att)), _full((D_MODEL, N_GROUPS * d_att)),
        ],
        out_specs=out_specs,
        out_shape=out_shape,
        scratch_shapes=[pltpu.VMEM((d_att // LANES, tm, LANES), F32)] if dilated else [],
        compiler_params=_params(("arbitrary",)),
        name="qkv_rope",
    )(x, cos, sin, w_kv, w_q)


ATTN_STEP_ROWS = 512
ATTN_SUB_ROWS = 128


def _attn_prompt_kernel(q_ref, kp_ref, kc_ref, vp_ref, vc_ref, o_ref, lse_ref, kx_ref, vx_ref, *, rows, sub):
    blk = pl.program_id(1)
    back = KEYS_PER_GROUP
    span = back + sub
    kx_ref[0:back, :] = kp_ref[...]
    kx_ref[back:back + rows, :] = kc_ref[...]
    vx_ref[0:back, :] = vp_ref[...]
    vx_ref[back:back + rows, :] = vc_ref[...]
    a = lax.broadcasted_iota(jnp.int32, (sub, span), 0)
    c = lax.broadcasted_iota(jnp.int32, (sub, span), 1)
    band = (c >= a) & (c <= a + back)
    lane = lax.broadcasted_iota(jnp.int32, (sub, LANES), 1)
    low = lane < HEAD_DIM

    for u in range(rows // sub):
        r0 = u * sub
        valid = band & ((blk > 0) | (c + r0 >= back)) if r0 < back else band
        lse_all = jnp.zeros((sub, LANES), F32)
        for p in range(N_HEADS // 2):
            sl = slice(p * LANES, (p + 1) * LANES)
            qp = q_ref[pl.ds(r0, sub), sl]
            kk = kx_ref[pl.ds(r0, span), sl]
            vv = vx_ref[pl.ds(r0, span), sl]
            outs = []
            for j in range(2):
                qm = jnp.where(low if j == 0 else jnp.logical_not(low), qp, jnp.zeros_like(qp))
                s = lax.dot_general(qm, kk, (((1,), (1,)), ((), ())), preferred_element_type=F32)
                s = jnp.where(valid, s, NEG)
                m = jnp.max(s, axis=1, keepdims=True)
                e = jnp.exp(s - m)
                den = jnp.sum(e, axis=1, keepdims=True)
                outs.append(_dot(e.astype(BF16), vv) / den)
                lse_all = jnp.where(lane == 2 * p + j, m + jnp.log(den), lse_all)
            o_ref[pl.ds(r0, sub), sl] = jnp.where(low, outs[0], outs[1]).astype(BF16)
        lse_ref[0, pl.ds(r0, sub), :] = lse_all


def _attn_prompt_group(qg, kg, vg, g):
    dil = GROUPS[g][1]
    back = KEYS_PER_GROUP
    assert GROUPS[g][0] // dil == back
    d_att = N_HEADS * HEAD_DIM
    n_rows = kg.shape[0]
    rows = min(ATTN_STEP_ROWS, n_rows)
    sub = ATTN_SUB_ROWS
    assert kg.shape[1] == dil * d_att and n_rows % rows == 0 and rows % back == 0 and rows % sub == 0
    cur = pl.BlockSpec((rows, d_att), lambda r, b: (b, r))
    prev = pl.BlockSpec((back, d_att), lambda r, b: (jnp.maximum(b * (rows // back) - 1, 0), r))
    return pl.pallas_call(
        functools.partial(_attn_prompt_kernel, rows=rows, sub=sub),
        grid=(dil, n_rows // rows),
        in_specs=[cur, prev, cur, prev, cur],
        out_specs=[cur, pl.BlockSpec((1, rows, LANES), lambda r, b: (r, b, 0))],
        out_shape=[jax.ShapeDtypeStruct((n_rows, dil * d_att), BF16),
                   jax.ShapeDtypeStruct((dil, n_rows, LANES), F32)],
        scratch_shapes=[pltpu.VMEM((back + rows, d_att), BF16), pltpu.VMEM((back + rows, d_att), BF16)],
        compiler_params=_params(("arbitrary", "arbitrary")),
        name=f"attn_prompt_g{g}",
    )(qg, kg, kg, vg, vg)


SAMPLE_ROWS = 16


def _sample_valid(t_len, past):
    key = np.arange(past + LANES)
    valid = np.zeros((SAMPLE_ROWS, past + LANES), np.float32)
    for r in range(N_GROUPS * t_len):
        g, i = divmod(r, t_len)
        window, dil = GROUPS[g]
        dist = past + i - key
        ok = (dist >= 0) & (dist <= window) & (dist % dil == 0) & (key < past + t_len)
        assert ok.sum() == window // dil + 1
        valid[r] = ok
    return valid[:, :past], valid[:, past:]


def _group_reduce(x, op):
    quarter = x.shape[0] // 4
    return op(op(x, pltpu.roll(x, quarter, 0)), op(pltpu.roll(x, 2 * quarter, 0), pltpu.roll(x, 3 * quarter, 0)))


def _attn_sample_kernel(q_ref, kvn_ref, vc_ref, vn_ref, cache_ref, o_ref, *, t_len):
    d_att = N_HEADS * HEAD_DIM
    q = q_ref[0].astype(F32)
    head = lax.broadcasted_iota(jnp.int32, (N_HEADS, d_att), 0)
    lane = lax.broadcasted_iota(jnp.int32, (N_HEADS, d_att), 1)
    own = head == lane // HEAD_DIM
    blocks = []
    for r in range(SAMPLE_ROWS):
        g, i = divmod(r, t_len)
        if g < N_GROUPS:
            qr = q[i:i + 1, g * d_att:(g + 1) * d_att]
            blocks.append(jnp.where(own, jnp.broadcast_to(qr, (N_HEADS, d_att)), 0.0))
        else:
            blocks.append(jnp.zeros((N_HEADS, d_att), F32))
    qbd = jnp.concatenate(blocks, axis=0).astype(BF16)
    kvn = jnp.concatenate([kvn_ref[0], jnp.zeros((LANES - t_len, 2 * d_att), F32)], axis=0).astype(BF16)
    s_c = jnp.where(vc_ref[...] > 0.0, _dot(qbd, cache_ref[0, 0].astype(BF16)), NEG)
    s_n = jnp.where(vn_ref[...] > 0.0, lax.dot_general(qbd, kvn[:, :d_att], (((1,), (1,)), ((), ())),
                                                       preferred_element_type=F32), NEG)
    ones = jnp.ones((s_n.shape[0], LANES), F32)
    m = jnp.maximum(jnp.max(s_c, axis=1, keepdims=True), jnp.max(s_n, axis=1, keepdims=True))
    m = _group_reduce(m * ones, jnp.maximum)[:, 0:1]
    e_c, e_n = jnp.exp(s_c - m), jnp.exp(s_n - m)
    den = jnp.sum(e_c, axis=1, keepdims=True) + jnp.sum(e_n, axis=1, keepdims=True)
    den = _group_reduce(den * ones, jnp.add)[:, 0:1]
    o = lax.dot_general((e_c / den).astype(BF16), cache_ref[0, 1].astype(BF16), (((1,), (1,)), ((), ())),
                        preferred_element_type=F32)
    o = o + _dot((e_n / den).astype(BF16), kvn[:, d_att:])
    o = jnp.sum(jnp.where(own[None], o.reshape(SAMPLE_ROWS, N_HEADS, d_att), 0.0), axis=1)
    out = o[0:t_len]
    for g in range(1, N_GROUPS):
        out = out + o[g * t_len:(g + 1) * t_len]
    o_ref[0] = out


def _attn_sample(q, kv_new, cache_t):
    bsz, t_len, _ = q.shape
    past = cache_t.shape[-1]
    d_att = N_HEADS * HEAD_DIM
    assert N_GROUPS * t_len <= SAMPLE_ROWS and SAMPLE_ROWS % 4 == 0 and SAMPLE_ROWS // 4 == t_len
    assert PAST_LEN >= past and past >= max(w for w, _ in GROUPS)
    valid_c, valid_n = (np.repeat(v, N_HEADS, axis=0) for v in _sample_valid(t_len, past))
    return pl.pallas_call(
        functools.partial(_attn_sample_kernel, t_len=t_len),
        grid=(bsz,),
        in_specs=[
            pl.BlockSpec((1, t_len, N_GROUPS * d_att), lambda b: (b, 0, 0)),
            pl.BlockSpec((1, t_len, 2 * d_att), lambda b: (b, 0, 0)),
            _full(valid_c.shape), _full(valid_n.shape),
            pl.BlockSpec((1, 2, d_att, past), lambda b: (b, 0, 0, 0)),
        ],
        out_specs=pl.BlockSpec((1, t_len, d_att), lambda b: (b, 0, 0)),
        out_shape=jax.ShapeDtypeStruct((bsz, t_len, d_att), F32),
        compiler_params=_params(("arbitrary",)),
        name="attn_sample",
    )(q, kv_new, jnp.asarray(valid_c), jnp.asarray(valid_n), cache_t)


def _store_token_tiles(ref, x):
    n = x.shape[0]
    for s in range(TILE_ROWS):
        ref[pl.ds(s, n, stride=TILE_ROWS), :] = x[:, s * LANES:(s + 1) * LANES]


def _load_token_slab(ref, n, s):
    return ref[pl.ds(s, n, stride=TILE_ROWS), :]


def _token_rows(ref, t):
    return ref.at[pl.ds(pl.multiple_of(t * TILE_ROWS, TILE_ROWS), TILE_ROWS), :]


def _split3(w):
    hi = w.astype(BF16)
    r1 = w - hi.astype(F32)
    mid = r1.astype(BF16)
    lo = (r1 - mid.astype(F32)).astype(BF16)
    return hi, mid, lo


def _oproj_router_tail(att, x_ref, wo_ref, g_ref, b_ref, wrh_ref, wrl_ref, x3t_ref, route_ref, cnt_ref):
    tm = att.shape[0]

    @pl.when(pl.program_id(0) == 0)
    def _():
        cnt_ref[...] = jnp.zeros_like(cnt_ref)

    x3 = _ln(ALPHA * x_ref[...] + _dot(att.astype(BF16), wo_ref[...]), g_ref[...], b_ref[...])
    _store_token_tiles(x3t_ref, x3)
    xh = x3.astype(BF16)
    xl = (x3 - xh.astype(F32)).astype(BF16)
    logits = _dot(xh, wrh_ref[...]) + (_dot(xh, wrl_ref[...]) + _dot(xl, wrh_ref[...]))
    lane = lax.broadcasted_iota(jnp.int32, logits.shape, 1)
    logits = jnp.where(lane < N_EXPERTS, logits, NEG)
    v1 = jnp.max(logits, axis=1, keepdims=True)
    i1 = jnp.min(jnp.where(logits == v1, lane, LANES), axis=1, keepdims=True)
    rest = jnp.where(lane == i1, NEG, logits)
    v2 = jnp.max(rest, axis=1, keepdims=True)
    i2 = jnp.min(jnp.where(rest == v2, lane, LANES), axis=1, keepdims=True)
    e2 = jnp.exp(v2 - v1)
    den = 1.0 + e2
    sel = jnp.where((lane == i1) | (lane == i2), 1.0, 0.0)
    tri = jnp.where(lax.broadcasted_iota(jnp.int32, (tm, tm), 1) < lax.broadcasted_iota(jnp.int32, (tm, tm), 0),
                    1.0, 0.0).astype(BF16)
    before = _dot(tri, sel.astype(BF16)) + cnt_ref[0:1, :]
    rank1 = jnp.sum(jnp.where(lane == i1, before, 0.0), axis=1, keepdims=True)
    rank2 = jnp.sum(jnp.where(lane == i2, before, 0.0), axis=1, keepdims=True)
    cnt_ref[0:1, :] = cnt_ref[0:1, :] + jnp.sum(sel, axis=0, keepdims=True)
    fields = (i1.astype(F32), i2.astype(F32), 1.0 / den, e2 / den, rank1, rank2)
    route = jnp.zeros(logits.shape, F32)
    for k, val in enumerate(fields):
        route = jnp.where(lane == k, val, route)
    route_ref[...] = route


def _load_dilated(o_ref, lse_ref, o_scr, l_scr, dil):
    if dil == 1:
        return o_ref[...].astype(F32), lse_ref[0]
    d_att = N_HEADS * HEAD_DIM
    rows = o_ref.shape[0]
    n_slabs = d_att // LANES
    for r in range(dil):
        for c in range(n_slabs):
            col = r * d_att + c * LANES
            o_scr[c, pl.ds(r, rows, stride=dil), :] = o_ref[:, col:col + LANES].astype(F32)
        l_scr[pl.ds(r, rows, stride=dil), :] = lse_ref[r]
    return jnp.concatenate([o_scr[c] for c in range(n_slabs)], axis=1), l_scr[...]


def _oproj_prompt_kernel(o0_ref, l0_ref, o1_ref, l1_ref, o2_ref, l2_ref, ex_ref, x_ref, wo_ref, g_ref, b_ref,
                         wrh_ref, wrl_ref, x3t_ref, route_ref, cnt_ref, *scr):
    parts = [_load_dilated(o_ref, l_ref, scr[2 * g], scr[2 * g + 1], GROUPS[g][1])
             for g, (o_ref, l_ref) in enumerate(((o0_ref, l0_ref), (o1_ref, l1_ref), (o2_ref, l2_ref)))]
    lse = [l for _, l in parts]
    m = jnp.maximum(jnp.maximum(lse[0], lse[1]), lse[2])
    w = [jnp.exp(l - m) for l in lse]
    den = w[0] + w[1] + w[2]
    att = None
    for g, (o, _) in enumerate(parts):
        hi, mid, lo = _split3(w[g] / den)
        wexp = _dot(hi, ex_ref[...]) + (_dot(mid, ex_ref[...]) + _dot(lo, ex_ref[...]))
        term = wexp * o
        att = term if att is None else att + term
    _oproj_router_tail(att, x_ref, wo_ref, g_ref, b_ref, wrh_ref, wrl_ref, x3t_ref, route_ref, cnt_ref)


def _oproj_sample_kernel(att_ref, x_ref, wo_ref, g_ref, b_ref, wrh_ref, wrl_ref, x3t_ref, route_ref, cnt_ref):
    _oproj_router_tail(att_ref[...], x_ref, wo_ref, g_ref, b_ref, wrh_ref, wrl_ref, x3t_ref, route_ref, cnt_ref)


def _router_split(w_router):
    wr = jnp.zeros((D_MODEL, LANES), F32).at[:, :N_EXPERTS].set(w_router)
    hi = wr.astype(BF16)
    return hi, (wr - hi.astype(F32)).astype(BF16)


def _oproj_router(att_parts, x, w_o, g, b, w_router, *, tm=256):
    n = x.shape[0]
    tm = min(tm, n)
    assert n % tm == 0
    d_att = N_HEADS * HEAD_DIM
    wrh, wrl = _router_split(w_router)
    row_blk = lambda w: pl.BlockSpec((tm, w), lambda i: (i, 0))
    tail_specs = [row_blk(D_MODEL), _full((d_att, D_MODEL)), _full((1, D_MODEL)), _full((1, D_MODEL)),
                  _full((D_MODEL, LANES)), _full((D_MODEL, LANES))]
    tail_args = (x, w_o, g, b, wrh, wrl)
    scratch = []
    if len(att_parts) == 1:
        kern, specs, args = _oproj_sample_kernel, [row_blk(d_att)], tuple(att_parts)
    else:
        expand = (np.arange(LANES)[:, None] == np.arange(d_att)[None, :] // HEAD_DIM).astype(np.float32)
        kern, specs, args = _oproj_prompt_kernel, [], []
        for (_, dil), (o, lse) in zip(GROUPS, att_parts):
            specs += [pl.BlockSpec((tm // dil, dil * d_att), lambda i: (i, 0)),
                      pl.BlockSpec((dil, tm // dil, LANES), lambda i: (0, i, 0))]
            args += [o, lse]
            scratch += [pltpu.VMEM((d_att // LANES, tm, LANES), F32), pltpu.VMEM((tm, LANES), F32)]
        specs.append(_full((LANES, d_att)))
        args.append(jnp.asarray(expand, BF16))
    return pl.pallas_call(
        kern,
        grid=(n // tm,),
        in_specs=specs + tail_specs,
        out_specs=[pl.BlockSpec((tm * TILE_ROWS, LANES), lambda i: (i, 0)), row_blk(LANES), _full((TILE_ROWS, LANES))],
        out_shape=[jax.ShapeDtypeStruct((n * TILE_ROWS, LANES), F32), jax.ShapeDtypeStruct((n, LANES), F32),
                   jax.ShapeDtypeStruct((TILE_ROWS, LANES), F32)],
        scratch_shapes=scratch,
        compiler_params=_params(("arbitrary",)),
        name="oproj_router",
    )(*args, *tail_args)


def _route_plan(routes, counts, tile, n_tiles):
    cnt = [c[0, :N_EXPERTS].astype(jnp.int32) for c in counts]
    total = sum(cnt)
    padded = (total + tile - 1) // tile * tile
    ends = jnp.cumsum(padded)
    offs = ends - padded
    fill = jnp.concatenate([offs + total, ends]).astype(jnp.int32)
    pos = []
    for route, c in zip(routes, cnt):
        e = route[:, 0:2].astype(jnp.int32)
        pos.append((offs[e] + route[:, 4:6].astype(jnp.int32)).reshape(-1))
        offs = offs + c
    n_live = (ends[-1] // tile).astype(jnp.int32)
    tile_id = jnp.minimum(jnp.arange(n_tiles, dtype=jnp.int32), n_live - 1)
    tile_expert = jnp.sum((ends // tile)[None, :] <= tile_id[:, None], axis=1).astype(jnp.int32)
    return pos, tile_expert, n_live.reshape(1), fill


def _dispatch_kernel(pos_ref, fill_ref, xa_ref, xb_ref, xs_ref, z_ref, sem, *, tm, steps_a, n_b, tile, n_tiles):
    i = pl.program_id(0)

    def scatter(src_ref, count, first_token):
        def row_copy(r, s):
            dst = _token_rows(xs_ref, pos_ref[(first_token + r) * 2 + s])
            return pltpu.make_async_copy(_token_rows(src_ref, r), dst, sem)

        def issue(r, c):
            row_copy(r, 0).start(priority=0)
            row_copy(r, 1).start(priority=1)
            return c

        def drain(r, c):
            row_copy(r, 0).wait()
            row_copy(r, 1).wait()
            return c

        lax.fori_loop(0, count, issue, 0)
        lax.fori_loop(0, count, drain, 0)

    @pl.when(i < steps_a)
    def _():
        scatter(xa_ref, tm, i * tm)

    @pl.when(i == steps_a)
    def _():
        scatter(xb_ref, n_b, steps_a * tm)
        z_ref[...] = jnp.zeros_like(z_ref)
        zero_row = lambda r: pltpu.make_async_copy(z_ref.at[0:TILE_ROWS, :], _token_rows(xs_ref, r), sem)
        tile_rows = tile * TILE_ROWS
        zero_tile = lambda j: pltpu.make_async_copy(
            z_ref, xs_ref.at[pl.ds(pl.multiple_of(j * tile_rows, tile_rows), tile_rows), :], sem)

        def loop(lo, hi, copy):
            def issue(r, c):
                copy(r).start()
                return c

            def drain(r, c):
                copy(r).wait()
                return c

            lax.fori_loop(lo, hi, issue, 0)
            lax.fori_loop(lo, hi, drain, 0)

        for e in range(N_EXPERTS):
            loop(fill_ref[e], fill_ref[N_EXPERTS + e], zero_row)
        loop(fill_ref[2 * N_EXPERTS - 1] // tile, n_tiles, zero_tile)


def _dispatch(pos, fill_rng, x3t_a, x3t_b, n_tiles, tile, *, tm=512):
    n_a, n_b = x3t_a.shape[0] // TILE_ROWS, x3t_b.shape[0] // TILE_ROWS
    tm = min(tm, n_a)
    assert n_a % tm == 0
    steps_a = n_a // tm
    return pl.pallas_call(
        functools.partial(_dispatch_kernel, tm=tm, steps_a=steps_a, n_b=n_b, tile=tile, n_tiles=n_tiles),
        grid_spec=pltpu.PrefetchScalarGridSpec(
            num_scalar_prefetch=2, grid=(steps_a + 1,),
            in_specs=[pl.BlockSpec((tm * TILE_ROWS, LANES), lambda i, *_: (jnp.minimum(i, steps_a - 1), 0)),
                      pl.BlockSpec((n_b * TILE_ROWS, LANES), lambda i, *_: (0, 0))],
            out_specs=pl.BlockSpec(memory_space=pl.ANY),
            scratch_shapes=[pltpu.VMEM((tile * TILE_ROWS, LANES), F32), pltpu.SemaphoreType.DMA(())]),
        out_shape=jax.ShapeDtypeStruct((n_tiles * tile * TILE_ROWS, LANES), F32),
        compiler_params=_params(("arbitrary",)),
        name="moe_dispatch",
    )(pos, fill_rng, x3t_a, x3t_b)


def _moe_ffn_kernel(te_ref, nl_ref, xs_ref, wg_ref, wu_ref, wd_ref, ys_ref, acc_ref, xb_ref, *, tile):
    j, f = pl.program_id(0), pl.program_id(1)

    @pl.when(j < nl_ref[0])
    def _():
        @pl.when(f == 0)
        def _():
            for s in range(TILE_ROWS):
                xb_ref[:, s * LANES:(s + 1) * LANES] = _load_token_slab(xs_ref, tile, s).astype(BF16)

        xb = xb_ref[...]
        h = _silu(_dot(xb, wg_ref[...])) * _dot(xb, wu_ref[...])
        y = _dot(h.astype(BF16), wd_ref[...])

        @pl.when(f == 0)
        def _():
            acc_ref[...] = y

        @pl.when(f > 0)
        def _():
            acc_ref[...] += y

        @pl.when(f == pl.num_programs(1) - 1)
        def _():
            _store_token_tiles(ys_ref, acc_ref[...])

    @pl.when((j >= nl_ref[0]) & (f == pl.num_programs(1) - 1))
    def _():
        ys_ref[...] = jnp.zeros_like(ys_ref)


def _moe_ffn(xs, tile_expert, n_live, wg, wu, wd, *, tile, n_f=2):
    n_tiles = tile_expert.shape[0]
    d_ff = wg.shape[2]
    tf = d_ff // n_f
    assert d_ff % n_f == 0 and tf % LANES == 0 and xs.shape[0] == n_tiles * tile * TILE_ROWS
    row_map = lambda j, f, te, nl: (jnp.minimum(j, nl[0] - 1), 0)
    f_idx = lambda j, f, nl: jnp.where(j < nl[0], f, n_f - 1)
    return pl.pallas_call(
        functools.partial(_moe_ffn_kernel, tile=tile),
        grid_spec=pltpu.PrefetchScalarGridSpec(
            num_scalar_prefetch=2, grid=(n_tiles, n_f),
            in_specs=[
                pl.BlockSpec((tile * TILE_ROWS, LANES), row_map),
                pl.BlockSpec((None, D_MODEL, tf), lambda j, f, te, nl: (te[j], 0, f_idx(j, f, nl))),
                pl.BlockSpec((None, D_MODEL, tf), lambda j, f, te, nl: (te[j], 0, f_idx(j, f, nl))),
                pl.BlockSpec((None, tf, D_MODEL), lambda j, f, te, nl: (te[j], f_idx(j, f, nl), 0)),
            ],
            out_specs=pl.BlockSpec((tile * TILE_ROWS, LANES), lambda j, f, te, nl: (j, 0)),
            scratch_shapes=[pltpu.VMEM((tile, D_MODEL), F32), pltpu.VMEM((tile, D_MODEL), BF16)]),
        out_shape=jax.ShapeDtypeStruct(xs.shape, F32),
        compiler_params=_params(("arbitrary", "arbitrary")),
        name="moe_ffn",
    )(tile_expert, n_live, xs, wg, wu, wd)


def _moe_combine_kernel(pos_ref, route_ref, x3t_ref, ys_ref, g_ref, b_ref, o_ref, y1_ref, y2_ref, pre_ref, sem, *, tm):
    i = pl.program_id(0)
    bufs = (y1_ref, y2_ref)

    def row_copy(r, s):
        return pltpu.make_async_copy(_token_rows(ys_ref, pos_ref[(i * tm + r) * 2 + s]), _token_rows(bufs[s], r), sem)

    def issue(r, c):
        row_copy(r, 0).start(priority=0)
        row_copy(r, 1).start(priority=1)
        return c

    def drain(r, c):
        row_copy(r, 0).wait()
        row_copy(r, 1).wait()
        return c

    lax.fori_loop(0, tm, issue, 0)
    lax.fori_loop(0, tm, drain, 0)
    route = route_ref[...]
    g1, g2 = route[:, 2:3], route[:, 3:4]
    for s in range(TILE_ROWS):
        y = g1 * _load_token_slab(y1_ref, tm, s) + g2 * _load_token_slab(y2_ref, tm, s)
        pre_ref[:, s * LANES:(s + 1) * LANES] = ALPHA * _load_token_slab(x3t_ref, tm, s) + y
    o_ref[...] = _ln(pre_ref[...], g_ref[...], b_ref[...])


def _moe_combine(pos, route, x3t, ys, g, b, *, tm=512):
    n = route.shape[0]
    tm = min(tm, n)
    assert n % tm == 0
    tok_buf = pltpu.VMEM((tm * TILE_ROWS, LANES), F32)
    return pl.pallas_call(
        functools.partial(_moe_combine_kernel, tm=tm),
        grid_spec=pltpu.PrefetchScalarGridSpec(
            num_scalar_prefetch=1, grid=(n // tm,),
            in_specs=[
                pl.BlockSpec((tm, LANES), lambda i, *_: (i, 0)),
                pl.BlockSpec((tm * TILE_ROWS, LANES), lambda i, *_: (i, 0)),
                pl.BlockSpec(memory_space=pl.ANY),
                pl.BlockSpec((1, D_MODEL), lambda i, *_: (0, 0)),
                pl.BlockSpec((1, D_MODEL), lambda i, *_: (0, 0)),
            ],
            out_specs=pl.BlockSpec((tm, D_MODEL), lambda i, *_: (i, 0)),
            scratch_shapes=[tok_buf, tok_buf, pltpu.VMEM((tm, D_MODEL), F32), pltpu.SemaphoreType.DMA(())]),
        out_shape=jax.ShapeDtypeStruct((n, D_MODEL), F32),
        compiler_params=_params(("arbitrary",)),
        name="moe_combine",
    )(pos, route, x3t, ys, g, b)


def kernel(x_prompt, x_sample, cache_kv, state_conv, conv_w_in, conv_b_in, conv_w_dw, conv_b_dw, conv_ln_g,
           conv_ln_b, conv_w_out, conv_b_out, ffn_w_gate, ffn_w_up, ffn_w_down, w_kv, attn_w_q, attn_w_o,
           moe_w_router, moe_w_gate, moe_w_up, moe_w_down, ln_g, ln_b):
    bp, t_p, _ = x_prompt.shape
    bs, t_s, _ = x_sample.shape
    assert bp == 1 and conv_w_in.shape[0] == 1 and attn_w_q.shape[0] == 1
    d_att = N_HEADS * HEAD_DIM
    row = lambda a: a.reshape(1, -1)
    bf = lambda a: a.astype(BF16)

    conv_w = (bf(conv_w_in[0]), row(conv_b_in[0]), conv_w_dw[0], row(conv_b_dw[0]), row(conv_ln_g[0]),
              row(conv_ln_b[0]), bf(conv_w_out[0]), row(conv_b_out[0]), row(ln_g[0, 0]), row(ln_b[0, 0]))
    ffn_w = (bf(ffn_w_gate[0]), bf(ffn_w_up[0]), bf(ffn_w_down[0]), row(ln_g[0, 1]), row(ln_b[0, 1]))
    wkv_b, wq_b, wo_b = bf(w_kv), bf(attn_w_q[0]), bf(attn_w_o[0])
    ln10 = (row(ln_g[1, 0]), row(ln_b[1, 0]))
    ln11 = (row(ln_g[1, 1]), row(ln_b[1, 1]))

    x1, conv_p = _conv_layer_prompt(x_prompt, jnp.zeros((bp, HALO, D_MODEL), F32), *conv_w)
    x2 = _ffn_layer(x1.reshape(t_p, D_MODEL), *ffn_w)
    kv_p, *qkv_views = _qkv_layer(x2, jnp.arange(t_p, dtype=jnp.int32), wkv_b, wq_b, dilated=True)
    parts = [_attn_prompt_group(*qkv_views[3 * g:3 * g + 3], g) for g in range(N_GROUPS)]
    x3t_p, route_p, cnt_p = _oproj_router(parts, x2, wo_b, *ln10, moe_w_router[0])
    n_keep = min(max(w for w, _ in GROUPS), t_p)
    kv_prompt = kv_p[t_p - n_keep:].reshape(bp, n_keep, 2, N_HEADS, HEAD_DIM)

    n_s = bs * t_s
    x1s, conv_s = _conv_layer_sample(x_sample, state_conv[0], *conv_w)
    x2s = _ffn_layer(x1s.reshape(n_s, D_MODEL), *ffn_w)
    pos_s = PAST_LEN + jnp.tile(jnp.arange(t_s, dtype=jnp.int32), bs)
    kv_s, q_s = _qkv_layer(x2s, pos_s, wkv_b, wq_b, dilated=False)
    att_s = _attn_sample(q_s.reshape(bs, t_s, N_GROUPS * d_att), kv_s.reshape(bs, t_s, 2 * d_att),
                         jnp.transpose(cache_kv, (0, 2, 3, 4, 1)).reshape(bs, 2, d_att, cache_kv.shape[1]))
    x3t_s, route_s, cnt_s = _oproj_router((att_s.reshape(n_s, d_att),), x2s, wo_b, *ln10, moe_w_router[0])
    kv_sample = kv_s.reshape(bs, t_s, 2, N_HEADS, HEAD_DIM)

    n_slots = 2 * (t_p + n_s)
    n_tiles = (n_slots + N_EXPERTS * (MOE_TILE - 1)) // MOE_TILE
    (pos_p, pos_sm), tile_expert, n_live, fill_rng = _route_plan((route_p, route_s), (cnt_p, cnt_s), MOE_TILE, n_tiles)
    xs = _dispatch(jnp.concatenate([pos_p, pos_sm]), fill_rng, x3t_p, x3t_s, n_tiles, MOE_TILE)
    ys = _moe_ffn(xs, tile_expert, n_live, bf(moe_w_gate[0]), bf(moe_w_up[0]), bf(moe_w_down[0]), tile=MOE_TILE)
    y_prompt = _moe_combine(pos_p, route_p, x3t_p, ys, *ln11).reshape(bp, t_p, D_MODEL)
    y_sample = _moe_combine(pos_sm, route_s, x3t_s, ys, *ln11).reshape(bs, t_s, D_MODEL)

    return (y_prompt, y_sample, conv_p[None], conv_s[None], kv_prompt, kv_sample)
```
